```python
import jax, jax.numpy as jnp
from jax import lax
import numpy as np

D_MODEL = 1024
BATCH = 8
SEQ = 2048
DEPTH = 1
DEC_BATCH = 128
DEC_SEQ = 4
PAST_LEN = 16384
PAGE_SIZE = 128

RWKV_HEAD = 64
D_RWKV = D_MODEL // 2
N_RWKV_HEADS = D_RWKV // RWKV_HEAD
DECAY_LORA = 64
AAA_LORA = 64
GATE_LORA = 128
D_SHIFT = 3 * D_RWKV + DECAY_LORA + AAA_LORA + GATE_LORA
D_POOL = D_MODEL - D_RWKV
POOL_WINDOWS = (2, 4, 8, 16)
N_POOL_GROUPS = len(POOL_WINDOWS)
POOL_GROUP = D_POOL // N_POOL_GROUPS
POOL_BUF = max(POOL_WINDOWS) - 1
D_IN = D_SHIFT + D_POOL
N_KEYS = 128
N_EXPERTS = N_KEYS * N_KEYS
PEER_HEADS = 8
PEER_TOPK = 16
PEER_DK = 256
PEER_DK_HALF = PEER_DK // 2
PEER_BLOCK = 128
PLE_DIM = 256
NORM_EPS = 1e-6
LNX_EPS = 64e-5

kernel_name = 'hybrid_rwkv7_pool_peer_step'


def rmsnorm(x, g):
    xf = x.astype(jnp.float32)
    y = xf * lax.rsqrt(jnp.mean(xf * xf, axis=-1, keepdims=True) + NORM_EPS)
    return (y * g.astype(jnp.float32)).astype(x.dtype)


def wkv_scan(s0, r, w, k, v, kk, a):
    def step(s, inp):
        r_t, w_t, k_t, v_t, kk_t, a_t = inp
        s_kk = jnp.einsum('bhij,bhj->bhi', s, -kk_t)
        s = (s * w_t[:, :, None, :]
             + s_kk[..., None] * (kk_t * a_t)[:, :, None, :]
             + v_t[..., None] * k_t[:, :, None, :])
        return s, jnp.einsum('bhij,bhj->bhi', s, r_t)
    xs = tuple(jnp.moveaxis(t.astype(jnp.float32), 1, 0) for t in (r, w, k, v, kk, a))
    s_final, out = lax.scan(step, s0.astype(jnp.float32), xs)
    return jnp.moveaxis(out, 0, 1), s_final


def rwkv7_mix(z, shift_prev, s0, mu, decay_w0, decay_b, a_0, a_b, g_b, k_k, k_a, r_k, lnx_g, lnx_b):
    B, T, _ = z.shape
    z_prev = jnp.concatenate([shift_prev[:, None, :].astype(z.dtype), z[:, :-1]], axis=1)
    zs = z + (z_prev - z) * mu
    r = zs[..., :D_RWKV]
    k = zs[..., D_RWKV:2 * D_RWKV]
    v = zs[..., 2 * D_RWKV:3 * D_RWKV]
    o = 3 * D_RWKV
    zw = zs[..., o:o + DECAY_LORA]
    za = zs[..., o + DECAY_LORA:o + DECAY_LORA + AAA_LORA]
    zg = zs[..., o + DECAY_LORA + AAA_LORA:]
    w_log = -jax.nn.softplus(-(decay_w0 + jnp.tanh(zw) @ decay_b).astype(jnp.float32)) - 0.5
    decay = jnp.exp(-jnp.exp(w_log))
    a = jax.nn.sigmoid(a_0 + za @ a_b)
    g = jax.nn.sigmoid(zg) @ g_b

    def heads(t):
        return t.reshape(B, T, N_RWKV_HEADS, RWKV_HEAD)

    r, k, v, a, decay = heads(r), heads(k), heads(v), heads(a), heads(decay)
    kkf = (k * k_k.reshape(N_RWKV_HEADS, RWKV_HEAD)).astype(jnp.float32)
    kk = kkf / jnp.maximum(jnp.sqrt(jnp.sum(kkf * kkf, axis=-1, keepdims=True)), 1e-12)
    k = k * (1 + (a - 1) * k_a.reshape(N_RWKV_HEADS, RWKV_HEAD))
    out, s_final = wkv_scan(s0, r, decay, k, v, kk, a)
    mean = jnp.mean(out, axis=-1, keepdims=True)
    var = jnp.mean(jnp.square(out - mean), axis=-1, keepdims=True)
    out = ((out - mean) * lax.rsqrt(var + LNX_EPS)).reshape(B, T, D_RWKV)
    out = out * lnx_g.astype(jnp.float32) + lnx_b.astype(jnp.float32)
    bonus = jnp.sum((r * k * r_k).astype(jnp.float32), axis=-1, keepdims=True) * v.astype(jnp.float32)
    y = (out + bonus.reshape(B, T, D_RWKV)) * g.astype(jnp.float32)
    return y.astype(z.dtype), z[:, -1], s_final


def pool_mix(u, buf, start_pos, pool_w, pool_scale):
    B, T, _ = u.shape
    ext = jnp.concatenate([buf.astype(u.dtype), u], axis=1)
    c = jnp.cumsum(ext.astype(jnp.float32), axis=1)
    c = jnp.concatenate([jnp.zeros((B, 1, D_POOL), jnp.float32), c], axis=1)
    pos = start_pos + jnp.arange(T, dtype=jnp.int32)
    lo = POOL_BUF + 1
    diffs = []
    for gi, wdw in enumerate(POOL_WINDOWS):
        sl = slice(gi * POOL_GROUP, (gi + 1) * POOL_GROUP)
        s = c[:, lo:lo + T, sl] - c[:, lo - wdw:lo - wdw + T, sl]
        cnt = jnp.minimum(pos + 1, wdw).astype(jnp.float32)
        diffs.append(s / cnt[None, :, None] - u[:, :, sl].astype(jnp.float32))
    pooled = jnp.stack(diffs, axis=2)
    y = jnp.einsum('btgc,gcd->btgd', pooled, pool_w.astype(jnp.float32)).reshape(B, T, D_POOL)
    y = y * pool_scale.astype(jnp.float32)
    return y.astype(u.dtype), ext[:, -POOL_BUF:]


def peer_ffn(xn, wq, sub_keys, u_tab, v_tab):
    B, T, D = xn.shape
    flat = xn.reshape(-1, D)
    n = flat.shape[0]
    pad = (-n) % PEER_BLOCK
    blocks = jnp.pad(flat, ((0, pad), (0, 0))).reshape(-1, PEER_BLOCK, D)

    def block_fn(xb):
        q = (xb @ wq).reshape(PEER_BLOCK, PEER_HEADS, 2, PEER_DK_HALF)
        s1 = jnp.einsum('thc,nc->thn', q[:, :, 0], sub_keys[0])
        s2 = jnp.einsum('thc,nc->thn', q[:, :, 1], sub_keys[1])
        v1, i1 = lax.top_k(s1, PEER_TOPK)
        v2, i2 = lax.top_k(s2, PEER_TOPK)
        cand = (v1[..., :, None] + v2[..., None, :]).reshape(PEER_BLOCK, PEER_HEADS, PEER_TOPK * PEER_TOPK)
        sc, ci = lax.top_k(cand, PEER_TOPK)
        e1 = jnp.take_along_axis(i1, ci // PEER_TOPK, axis=-1)
        e2 = jnp.take_along_axis(i2, ci % PEER_TOPK, axis=-1)
        expert = e1 * N_KEYS + e2
        gate = jax.nn.softmax(sc.astype(jnp.float32), axis=-1)
        ue = jnp.take(u_tab, expert, axis=0)
        act = jax.nn.gelu(jnp.einsum('thkd,td->thk', ue, xb).astype(jnp.float32), approximate=False) * gate
        ve = jnp.take(v_tab, expert, axis=0)
        return jnp.einsum('thk,thkd->td', act.astype(xb.dtype), ve)

    out = lax.map(block_fn, blocks).reshape(-1, D)[:n]
    return out.reshape(B, T, D)


def hybrid_layer(x, p, start_pos, shift_prev, wkv_prev, pool_prev, lw):
    n1 = rmsnorm(x, lw['norm_mix_g'])
    z = n1 @ lw['w_in']
    y_r, shift_new, wkv_new = rwkv7_mix(
        z[..., :D_SHIFT], shift_prev, wkv_prev, lw['shift_mu'], lw['decay_w0'], lw['decay_b'],
        lw['a_0'], lw['a_b'], lw['g_b'], lw['k_k'], lw['k_a'], lw['r_k'], lw['lnx_g'], lw['lnx_b'])
    y_p, pool_new = pool_mix(z[..., D_SHIFT:], pool_prev, start_pos, lw['pool_w'], lw['pool_scale'])
    h = x + jnp.concatenate([y_r, y_p], axis=-1) @ lw['w_out']
    h = h + peer_ffn(rmsnorm(h, lw['norm_ffn_g']), lw['peer_wq'], lw['peer_keys'], lw['peer_u'], lw['peer_v'])
    gate = jax.nn.sigmoid((rmsnorm(h, lw['norm_ple_g']) @ lw['ple_gate_w']).astype(jnp.float32))
    h = h + ((p @ lw['ple_w']).astype(jnp.float32) * gate).astype(h.dtype)
    return h, shift_new, wkv_new, pool_new


def setup_inputs(seed: int = 0) -> dict:
    key = jax.random.key(seed)
    ks = list(jax.random.split(key, 40))

    def nrm(shape, scale):
        return scale * jax.random.normal(ks.pop(), shape, jnp.float32)

    def unif(shape, lo, hi):
        return jax.random.uniform(ks.pop(), shape, jnp.float32, lo, hi)

    L = DEPTH
    return {
        'x_prompt': nrm((BATCH, SEQ, D_MODEL), 1.0),
        'x_sample': nrm((DEC_BATCH, DEC_SEQ, D_MODEL), 1.0),
        'state_shift': nrm((L, DEC_BATCH, D_SHIFT), 1.0),
        'state_wkv': nrm((L, DEC_BATCH, N_RWKV_HEADS, RWKV_HEAD, RWKV_HEAD), 0.3),
        'state_pool': nrm((L, DEC_BATCH, POOL_BUF, D_POOL), 1.0),
        'p_prompt': nrm((L, BATCH, SEQ, PLE_DIM), 1.0),
        'p_sample': nrm((L, DEC_BATCH, DEC_SEQ, PLE_DIM), 1.0),
        'norm_mix_g': 1.0 + nrm((L, D_MODEL), 0.02),
        'w_in': nrm((L, D_MODEL, D_IN), D_MODEL ** -0.5),
        'shift_mu': unif((L, D_SHIFT), 0.0, 1.0),
        'decay_w0': unif((L, D_RWKV), -4.0, 0.0),
        'decay_b': nrm((L, DECAY_LORA, D_RWKV), 0.1 * DECAY_LORA ** -0.5),
        'a_0': nrm((L, D_RWKV), 0.1),
        'a_b': nrm((L, AAA_LORA, D_RWKV), 0.5 * AAA_LORA ** -0.5),
        'g_b': nrm((L, GATE_LORA, D_RWKV), GATE_LORA ** -0.5),
        'k_k': 0.85 + nrm((L, D_RWKV), 0.05),
        'k_a': 1.0 + nrm((L, D_RWKV), 0.05),
        'r_k': nrm((L, N_RWKV_HEADS, RWKV_HEAD), 0.1),
        'lnx_g': 1.0 + nrm((L, D_RWKV), 0.02),
        'lnx_b': nrm((L, D_RWKV), 0.01),
        'pool_w': nrm((L, N_POOL_GROUPS, POOL_GROUP, POOL_GROUP), POOL_GROUP ** -0.5),
        'pool_scale': 1.0 + nrm((L, D_POOL), 0.02),
        'w_out': nrm((L, D_MODEL, D_MODEL), 0.5 * D_MODEL ** -0.5),
        'norm_ffn_g': 1.0 + nrm((L, D_MODEL), 0.02),
        'peer_wq': nrm((L, D_MODEL, PEER_HEADS * PEER_DK), D_MODEL ** -0.5),
        'peer_keys': nrm((L, 2, N_KEYS, PEER_DK_HALF), PEER_DK_HALF ** -0.5),
        'peer_u': nrm((L, N_EXPERTS, D_MODEL), D_MODEL ** -0.5),
        'peer_v': nrm((L, N_EXPERTS, D_MODEL), 0.2),
        'norm_ple_g': 1.0 + nrm((L, D_MODEL), 0.02),
        'ple_w': nrm((L, PLE_DIM, D_MODEL), PLE_DIM ** -0.5),
        'ple_gate_w': nrm((L, D_MODEL, D_MODEL), D_MODEL ** -0.5),
        'final_norm_g': 1.0 + nrm((D_MODEL,), 0.02),
    }


def reference(x_prompt, x_sample, state_shift, state_wkv, state_pool, p_prompt, p_sample,
              norm_mix_g, w_in, shift_mu, decay_w0, decay_b, a_0, a_b, g_b, k_k, k_a, r_k,
              lnx_g, lnx_b, pool_w, pool_scale, w_out, norm_ffn_g, peer_wq, peer_keys,
              peer_u, peer_v, norm_ple_g, ple_w, ple_gate_w, final_norm_g):
    hp, hs = x_prompt, x_sample
    sh_p, wk_p, po_p, sh_s, wk_s, po_s = [], [], [], [], [], []
    for i in range(DEPTH):
        lw = dict(norm_mix_g=norm_mix_g[i], w_in=w_in[i], shift_mu=shift_mu[i], decay_w0=decay_w0[i],
                  decay_b=decay_b[i], a_0=a_0[i], a_b=a_b[i], g_b=g_b[i], k_k=k_k[i], k_a=k_a[i],
                  r_k=r_k[i], lnx_g=lnx_g[i], lnx_b=lnx_b[i], pool_w=pool_w[i], pool_scale=pool_scale[i],
                  w_out=w_out[i], norm_ffn_g=norm_ffn_g[i], peer_wq=peer_wq[i], peer_keys=peer_keys[i],
                  peer_u=peer_u[i], peer_v=peer_v[i], norm_ple_g=norm_ple_g[i], ple_w=ple_w[i],
                  ple_gate_w=ple_gate_w[i])
        hp, s1, s2, s3 = hybrid_layer(
            hp, p_prompt[i], 0,
            jnp.zeros((BATCH, D_SHIFT), hp.dtype),
            jnp.zeros((BATCH, N_RWKV_HEADS, RWKV_HEAD, RWKV_HEAD), jnp.float32),
            jnp.zeros((BATCH, POOL_BUF, D_POOL), hp.dtype), lw)
        sh_p.append(s1)
        wk_p.append(s2.astype(state_wkv.dtype))
        po_p.append(s3)
        hs, t1, t2, t3 = hybrid_layer(hs, p_sample[i], PAST_LEN, state_shift[i], state_wkv[i], state_pool[i], lw)
        sh_s.append(t1)
        wk_s.append(t2.astype(state_wkv.dtype))
        po_s.append(t3)
    y_prompt = rmsnorm(hp, final_norm_g)
    y_sample = rmsnorm(hs, final_norm_g)
    return (y_prompt, y_sample, jnp.stack(sh_p), jnp.stack(wk_p), jnp.stack(po_p),
            jnp.stack(sh_s), jnp.stack(wk_s), jnp.stack(po_s))
```

```python
import functools
import math

import jax
import jax.numpy as jnp
from jax import lax
from jax.experimental import pallas as pl
from jax.experimental.pallas import tpu as pltpu

F32 = jnp.float32
BF16 = jnp.bfloat16

D_MODEL = 1024
RWKV_HEAD = 64
D_RWKV = 512
N_RWKV_HEADS = 8
DECAY_LORA = 64
AAA_LORA = 64
GATE_LORA = 128
D_SHIFT = 3 * D_RWKV + DECAY_LORA + AAA_LORA + GATE_LORA
D_POOL = 512
POOL_WINDOWS = (2, 4, 8, 16)
POOL_GROUP = 128
POOL_BUF = 15
D_IN = D_SHIFT + D_POOL
N_KEYS = 128
N_EXPERTS = N_KEYS * N_KEYS
PEER_HEADS = 8
PEER_TOPK = 16
PEER_DK = 256
PEER_DK_HALF = 128
PLE_DIM = 256
NORM_EPS = 1e-6
LNX_EPS = 64e-5
PAST_LEN = 16384

HALO = 16
LANES = 128
SUBLANES = 8
VMEM_LIMIT = 48 * 1024 * 1024

NEG_INF = float("-inf")


def _cparams(*sem):
    return pltpu.CompilerParams(dimension_semantics=sem, vmem_limit_bytes=VMEM_LIMIT)


def _rms(x, g):
    return x * lax.rsqrt(jnp.mean(x * x, axis=-1, keepdims=True) + NORM_EPS) * g


def _sigmoid(x):
    return 1.0 / (1.0 + jnp.exp(-x))


def _bdot(a, b):
    return jnp.dot(a.astype(BF16), b.astype(BF16), preferred_element_type=F32)


def _seg_sum(x, bd):
    x1 = x.astype(BF16)
    r1 = x - x1.astype(F32)
    x2 = r1.astype(BF16)
    x3 = (r1 - x2.astype(F32)).astype(BF16)
    dot = functools.partial(jnp.dot, preferred_element_type=F32)
    return dot(x1, bd) + dot(x2, bd) + dot(x3, bd)


def _proj_body(x_ref, g_ref, w_ref, z_ref):
    z_ref[...] = _bdot(_rms(x_ref[...], g_ref[...]), w_ref[...])


def _proj(x2d, g, w_bf16, tile):
    rows = x2d.shape[0]
    return pl.pallas_call(
        _proj_body,
        grid=(rows // tile,),
        in_specs=[
            pl.BlockSpec((tile, D_MODEL), lambda i: (i, 0)),
            pl.BlockSpec((1, D_MODEL), lambda i: (0, 0)),
            pl.BlockSpec((D_MODEL, D_IN), lambda i: (0, 0)),
        ],
        out_specs=pl.BlockSpec((tile, D_IN), lambda i: (i, 0)),
        out_shape=jax.ShapeDtypeStruct((rows, D_IN), F32),
        compiler_params=_cparams("parallel"),
        name="proj",
    )(x2d, g, w_bf16)


def _premix_body(z_ref, halo_ref, mu_ref, w0_ref, decb_ref, a0_ref, ab_ref, gb_ref, kk_ref, ka_ref,
                 rk_ref, poolw_ref, pools_ref, bd_ref,
                 r_o, w_o, k_o, v_o, kkn_o, b_o, g_o, bonus_o, yp_o, ext_scr, *, tile, seq_tiles):
    halo = halo_ref[...]
    if seq_tiles:
        ti = pl.program_id(0) % seq_tiles
        halo = jnp.where(ti == 0, 0.0, halo)
    ext_scr[0:HALO, :] = halo
    ext_scr[HALO:HALO + tile, :] = z_ref[...]

    zr = z_ref[:, :D_SHIFT]
    zp = ext_scr[HALO - 1:HALO - 1 + tile, :D_SHIFT]
    zs = zr + (zp - zr) * mu_ref[...]
    r = zs[:, :D_RWKV]
    k = zs[:, D_RWKV:2 * D_RWKV]
    v = zs[:, 2 * D_RWKV:3 * D_RWKV]
    o = 3 * D_RWKV
    zw = zs[:, o:o + DECAY_LORA]
    za = zs[:, o + DECAY_LORA:o + DECAY_LORA + AAA_LORA]
    zg = zs[:, o + DECAY_LORA + AAA_LORA:]
    dec_in = w0_ref[...] + _bdot(jnp.tanh(zw), decb_ref[...])
    nx = -dec_in
    softplus = jnp.maximum(nx, 0.0) + jnp.log1p(jnp.exp(-jnp.abs(nx)))
    decay = jnp.exp(-jnp.exp(-softplus - 0.5))
    a = _sigmoid(a0_ref[...] + _bdot(za, ab_ref[...]))
    g = _bdot(_sigmoid(zg), gb_ref[...])
    bd = bd_ref[...]
    kkf = k * kk_ref[...]
    kkn = kkf / jnp.maximum(jnp.sqrt(_seg_sum(kkf * kkf, bd)), 1e-12)
    k2 = k * (1.0 + (a - 1.0) * ka_ref[...])
    bonus = _seg_sum(r * k2 * rk_ref[...], bd) * v
    r_o[...] = r
    w_o[...] = decay
    k_o[...] = k2
    v_o[...] = v
    kkn_o[...] = kkn
    b_o[...] = kkn * a
    g_o[...] = g
    bonus_o[...] = bonus

    if seq_tiles:
        pos = ti * tile + lax.broadcasted_iota(jnp.int32, (tile, 1), 0)
    for gi, wdw in enumerate(POOL_WINDOWS):
        cols = slice(D_SHIFT + gi * POOL_GROUP, D_SHIFT + (gi + 1) * POOL_GROUP)
        u_ext = ext_scr[:, cols]
        s = u_ext
        shift = 1
        while shift < wdw:
            s = s + pltpu.roll(s, shift, 0)
            shift *= 2
        u = u_ext[HALO:, :]
        if seq_tiles:
            cnt = jnp.minimum(pos + 1, wdw).astype(F32)
        else:
            cnt = float(wdw)
        pooled = s[HALO:, :] / cnt - u
        y = _bdot(pooled, poolw_ref[gi])
        yp_o[:, gi * POOL_GROUP:(gi + 1) * POOL_GROUP] = y * pools_ref[:, gi * POOL_GROUP:(gi + 1) * POOL_GROUP]


def _premix(z, lw, bd, tile, seq_tiles):
    rows = z.shape[0]
    body = functools.partial(_premix_body, tile=tile, seq_tiles=seq_tiles)
    full = lambda shape: pl.BlockSpec(shape, lambda i: (0,) * len(shape))
    out_block = pl.BlockSpec((tile, D_RWKV), lambda i: (i, 0))
    out = jax.ShapeDtypeStruct((rows, D_RWKV), F32)
    halo_blocks = tile // HALO
    return pl.pallas_call(
        body,
        grid=(rows // tile,),
        in_specs=[
            pl.BlockSpec((tile, D_IN), lambda i: (i, 0)),
            pl.BlockSpec((HALO, D_IN), lambda i: (jnp.maximum(i * halo_blocks - 1, 0), 0)),
            full((1, D_SHIFT)), full((1, D_RWKV)), full((DECAY_LORA, D_RWKV)), full((1, D_RWKV)),
            full((AAA_LORA, D_RWKV)), full((GATE_LORA, D_RWKV)), full((1, D_RWKV)), full((1, D_RWKV)),
            full((1, D_RWKV)), full((4, POOL_GROUP, POOL_GROUP)), full((1, D_POOL)), full((D_RWKV, D_RWKV)),
        ],
        out_specs=[out_block] * 9,
        out_shape=[out] * 9,
        scratch_shapes=[pltpu.VMEM((tile + HALO, D_IN), F32)],
        compiler_params=_cparams("parallel"),
        name="premix",
    )(z, z, lw["shift_mu"], lw["decay_w0"], lw["decay_b"], lw["a_0"], lw["a_b"], lw["g_b"], lw["k_k"],
      lw["k_a"], lw["r_k"], lw["pool_w"], lw["pool_scale"], bd)


PAIRS = N_RWKV_HEADS // 2


def _scan_body(r_ref, w_ref, k_ref, kk_ref, b_ref, vt_ref, s0_ref, o_ref, sout_ref, st_scr, *, nb, tc):
    c = pl.program_id(1)

    @pl.when(c == 0)
    def _():
        st_scr[...] = s0_ref[...]

    lane = lax.broadcasted_iota(jnp.int32, (RWKV_HEAD, LANES), 1)
    row = lax.broadcasted_iota(jnp.int32, (RWKV_HEAD, LANES), 0)
    lo = lane < RWKV_HEAD
    diag = (lane & (RWKV_HEAD - 1)) == row
    lane_t = lax.broadcasted_iota(jnp.int32, (LANES, tc), 1)

    def half_sums(m):
        s_lo = jnp.sum(jnp.where(lo, m, 0.0), axis=-1, keepdims=True)
        s_hi = jnp.sum(jnp.where(lo, 0.0, m), axis=-1, keepdims=True)
        return jnp.where(lo, s_lo, s_hi)

    def step(t, carry):
        t_hot = lane_t == t
        for b in range(nb):
            for p in range(PAIRS):
                row_p = pl.ds(p, 1)
                kk = kk_ref[b, t, row_p, :]
                w = w_ref[b, t, row_p, :]
                bb = b_ref[b, t, row_p, :]
                k = k_ref[b, t, row_p, :]
                r = r_ref[b, t, row_p, :]
                s = st_scr[b, p]
                sk = half_sums(s * kk)
                vt = vt_ref[b, pl.ds(p * LANES, LANES), :]
                vc = jnp.sum(jnp.where(t_hot, vt, 0.0), axis=-1, keepdims=True)
                vcol = jnp.where(lo, vc[:RWKV_HEAD], vc[RWKV_HEAD:])
                s = s * w - sk * bb + vcol * k
                st_scr[b, p] = s
                oc = half_sums(s * r)
                o_ref[b, t, pl.ds(p, 1), :] = jnp.sum(jnp.where(diag, oc, 0.0), axis=0, keepdims=True)
        return carry

    if tc < SUBLANES:
        for t in range(tc):
            step(t, 0)
    else:
        lax.fori_loop(0, tc, step, 0)

    @pl.when(c == pl.num_programs(1) - 1)
    def _():
        sout_ref[...] = st_scr[...]


def _scan(r, w, k, kk, b, vt, s0, nb, tc):
    bsz, t_len, _ = r.shape
    body = functools.partial(_scan_body, nb=nb, tc=tc)
    seq = pl.BlockSpec((nb, tc, PAIRS, LANES), lambda i, c: (i, c, 0, 0))
    st = pl.BlockSpec((nb, PAIRS, RWKV_HEAD, LANES), lambda i, c: (i, 0, 0, 0))
    split = lambda a: a.reshape(bsz, t_len, PAIRS, LANES)
    o, s_fin = pl.pallas_call(
        body,
        grid=(bsz // nb, t_len // tc),
        in_specs=[seq, seq, seq, seq, seq, pl.BlockSpec((nb, D_RWKV, tc), lambda i, c: (i, 0, c)), st],
        out_specs=[seq, st],
        out_shape=[jax.ShapeDtypeStruct((bsz, t_len, PAIRS, LANES), F32),
                   jax.ShapeDtypeStruct((bsz, PAIRS, RWKV_HEAD, LANES), F32)],
        scratch_shapes=[pltpu.VMEM((nb, PAIRS, RWKV_HEAD, LANES), F32)],
        compiler_params=_cparams("parallel", "arbitrary"),
        name="scan",
    )(split(r), split(w), split(k), split(kk), split(b), vt, s0)
    return o.reshape(bsz, t_len, D_RWKV), s_fin


def _pair_state(s):
    bsz = s.shape[0]
    s = s.reshape(bsz, PAIRS, 2, RWKV_HEAD, RWKV_HEAD).transpose(0, 1, 3, 2, 4)
    return s.reshape(bsz, PAIRS, RWKV_HEAD, LANES)


def _unpair_state(s):
    bsz = s.shape[0]
    s = s.reshape(bsz, PAIRS, RWKV_HEAD, 2, RWKV_HEAD).transpose(0, 1, 3, 2, 4)
    return s.reshape(bsz, N_RWKV_HEADS, RWKV_HEAD, RWKV_HEAD)


def _postmix_body(x_ref, o_ref, bonus_ref, g_ref, yp_ref, lng_ref, lnb_ref, wout_ref, nfg_ref, bd_ref,
                  h_o, xnt_o):
    bd = bd_ref[...]
    o = o_ref[...]
    inv_n = 1.0 / RWKV_HEAD
    d = o - _seg_sum(o, bd) * inv_n
    var = _seg_sum(d * d, bd) * inv_n
    y_r = (d * lax.rsqrt(var + LNX_EPS) * lng_ref[...] + lnb_ref[...] + bonus_ref[...]) * g_ref[...]
    h = x_ref[...] + _bdot(y_r, wout_ref[:D_RWKV, :]) + _bdot(yp_ref[...], wout_ref[D_RWKV:, :])
    h_o[...] = h
    xnt_o[...] = _rms(h, nfg_ref[...]).T.astype(BF16)


def _postmix(x2d, o, bonus, g, yp, lw, w_out_bf16, bd, tile):
    rows = x2d.shape[0]
    full = lambda shape: pl.BlockSpec(shape, lambda i: (0,) * len(shape))
    half = pl.BlockSpec((tile, D_RWKV), lambda i: (i, 0))
    wide = pl.BlockSpec((tile, D_MODEL), lambda i: (i, 0))
    return pl.pallas_call(
        _postmix_body,
        grid=(rows // tile,),
        in_specs=[wide, half, half, half, half, full((1, D_RWKV)), full((1, D_RWKV)),
                  full((D_MODEL, D_MODEL)), full((1, D_MODEL)), full((D_RWKV, D_RWKV))],
        out_specs=[wide, pl.BlockSpec((D_MODEL, tile), lambda i: (0, i))],
        out_shape=[jax.ShapeDtypeStruct((rows, D_MODEL), F32), jax.ShapeDtypeStruct((D_MODEL, rows), BF16)],
        compiler_params=_cparams("parallel"),
        name="postmix",
    )(x2d, o, bonus, g, yp, lw["lnx_g"], lw["lnx_b"], w_out_bf16, lw["norm_ffn_g"], bd)


_CAND = [(a, b) for a in range(PEER_TOPK) for b in range(PEER_TOPK) if (a + 1) * (b + 1) <= PEER_TOPK]
_CAND_ROWS = -(-len(_CAND) // 8) * 8


def _top16(s):
    idx = lax.broadcasted_iota(jnp.int32, s.shape, 0)
    vals = []
    for _ in range(PEER_TOPK):
        m = jnp.max(s, axis=0, keepdims=True)
        vals.append(m)
        first = jnp.min(jnp.where(s == m, idx, s.shape[0]), axis=0, keepdims=True)
        s = jnp.where(idx == first, NEG_INF, s)
    return vals


def _route_body(xnt_ref, wqt_ref, keys_ref, s1_o, s2_o, e1_o, e2_o, tau_o, *, tile):
    qt = jnp.dot(wqt_ref[...], xnt_ref[...], preferred_element_type=F32).astype(BF16)
    cand_row = lax.broadcasted_iota(jnp.int32, (_CAND_ROWS, tile), 0)
    for h in range(PEER_HEADS):
        base = h * PEER_DK
        s1 = jnp.dot(keys_ref[0], qt[base:base + PEER_DK_HALF], preferred_element_type=F32)
        s2 = jnp.dot(keys_ref[1], qt[base + PEER_DK_HALF:base + PEER_DK], preferred_element_type=F32)
        v1 = _top16(s1)
        v2 = _top16(s2)
        cand = jnp.full((_CAND_ROWS, tile), NEG_INF, F32)
        for ci, (a, b) in enumerate(_CAND):
            cand = jnp.where(cand_row == ci, v1[a] + v2[b], cand)
        sc = _top16(cand)
        z = jnp.ones_like(sc[0])
        for kth in sc[1:]:
            z = z + jnp.exp(kth - sc[0])
        s1_o[h] = s1
        s2_o[h] = s2
        e1_o[h] = jnp.exp(s1 - v1[0])
        e2_o[h] = jnp.exp(s2 - v2[0]) / z
        tau_o[pl.ds(h, 1), :] = sc[PEER_TOPK - 1]


def _route(xnt, wqt_bf16, keys_bf16, tile):
    rows = xnt.shape[1]
    body = functools.partial(_route_body, tile=tile)
    sco = pl.BlockSpec((PEER_HEADS, N_KEYS, tile), lambda i: (0, 0, i))
    sc_shape = jax.ShapeDtypeStruct((PEER_HEADS, N_KEYS, rows), F32)
    return pl.pallas_call(
        body,
        grid=(rows // tile,),
        in_specs=[
            pl.BlockSpec((D_MODEL, tile), lambda i: (0, i)),
            pl.BlockSpec((PEER_HEADS * PEER_DK, D_MODEL), lambda i: (0, 0)),
            pl.BlockSpec((2, N_KEYS, PEER_DK_HALF), lambda i: (0, 0, 0)),
        ],
        out_specs=[sco, sco, sco, sco, pl.BlockSpec((PEER_HEADS, tile), lambda i: (0, i))],
        out_shape=[sc_shape, sc_shape, sc_shape, sc_shape, jax.ShapeDtypeStruct((PEER_HEADS, rows), F32)],
        compiler_params=_cparams("parallel"),
        name="route",
    )(xnt, wqt_bf16, keys_bf16)


E1_PER_CHUNK = 8
CHUNK = E1_PER_CHUNK * N_KEYS


def _gelu(x):
    return 0.5 * x * (1.0 + lax.erf(x * (1.0 / math.sqrt(2.0))))


def _peer_body(xnt_ref, u_ref, vt_ref, s1_ref, s2_ref, e1_ref, e2_ref, tau_ref, out_o, acc_scr, a_scr, *, tile):
    c = pl.program_id(1)

    @pl.when(c == 0)
    def _():
        acc_scr[...] = jnp.zeros_like(acc_scr)

    ht = jnp.dot(u_ref[...], xnt_ref[...], preferred_element_type=F32)
    for e in range(E1_PER_CHUNK):
        e1 = c * E1_PER_CHUNK + e
        wgt = jnp.zeros((N_KEYS, tile), F32)
        for h in range(PEER_HEADS):
            s1row = s1_ref[h, pl.ds(e1, 1), :]
            e1row = e1_ref[h, pl.ds(e1, 1), :]
            sel = (s1row + s2_ref[h]) >= tau_ref[pl.ds(h, 1), :]
            wgt = wgt + jnp.where(sel, e1row * e2_ref[h], 0.0)
        rows = slice(e * N_KEYS, (e + 1) * N_KEYS)
        a_scr[rows, :] = (_gelu(ht[rows, :]) * wgt).astype(BF16)
    acc_scr[...] += jnp.dot(vt_ref[...], a_scr[...], preferred_element_type=F32)

    @pl.when(c == pl.num_programs(1) - 1)
    def _():
        out_o[...] = acc_scr[...].T


def _peer(xnt, u_bf16, vt_bf16, s1, s2, e1, e2, tau, tile):
    rows = xnt.shape[1]
    body = functools.partial(_peer_body, tile=tile)
    sco = pl.BlockSpec((PEER_HEADS, N_KEYS, tile), lambda i, c: (0, 0, i))
    return pl.pallas_call(
        body,
        grid=(rows // tile, N_EXPERTS // CHUNK),
        in_specs=[
            pl.BlockSpec((D_MODEL, tile), lambda i, c: (0, i)),
            pl.BlockSpec((CHUNK, D_MODEL), lambda i, c: (c, 0)),
            pl.BlockSpec((D_MODEL, CHUNK), lambda i, c: (0, c)),
            sco, sco, sco, sco,
            pl.BlockSpec((PEER_HEADS, tile), lambda i, c: (0, i)),
        ],
        out_specs=pl.BlockSpec((tile, D_MODEL), lambda i, c: (i, 0)),
        out_shape=jax.ShapeDtypeStruct((rows, D_MODEL), F32),
        scratch_shapes=[pltpu.VMEM((D_MODEL, tile), F32), pltpu.VMEM((CHUNK, tile), BF16)],
        compiler_params=_cparams("parallel", "arbitrary"),
        name="peer",
    )(xnt, u_bf16, vt_bf16, s1, s2, e1, e2, tau)


def _final_body(h_ref, peer_ref, p_ref, npg_ref, gatew_ref, plew_ref, fg_ref, y_o):
    h = h_ref[...] + peer_ref[...]
    gate = _sigmoid(_bdot(_rms(h, npg_ref[...]), gatew_ref[...]))
    h = h + _bdot(p_ref[...], plew_ref[...]) * gate
    y_o[...] = _rms(h, fg_ref[...])


def _final(h, peer, p2d, lw, gate_w_bf16, ple_w_bf16, final_g, tile):
    rows = h.shape[0]
    full = lambda shape: pl.BlockSpec(shape, lambda i: (0,) * len(shape))
    wide = pl.BlockSpec((tile, D_MODEL), lambda i: (i, 0))
    return pl.pallas_call(
        _final_body,
        grid=(rows // tile,),
        in_specs=[wide, wide, pl.BlockSpec((tile, PLE_DIM), lambda i: (i, 0)), full((1, D_MODEL)),
                  full((D_MODEL, D_MODEL)), full((PLE_DIM, D_MODEL)), full((1, D_MODEL))],
        out_specs=wide,
        out_shape=jax.ShapeDtypeStruct((rows, D_MODEL), F32),
        compiler_params=_cparams("parallel"),
        name="final",
    )(h, peer, p2d, lw["norm_ple_g"], gate_w_bf16, ple_w_bf16, final_g)


def _block_diag_ones():
    seg = jnp.arange(D_RWKV) // RWKV_HEAD
    return (seg[:, None] == seg[None, :]).astype(BF16)


def _tail(x2d, p2d, o, bonus, g, yp, lw, wts, bd, final_g, tile):
    h, xnt = _postmix(x2d, o, bonus, g, yp, lw, wts["w_out"], bd, tile)
    s1, s2, e1, e2, tau = _route(xnt, wts["wqt"], wts["keys"], tile)
    peer = _peer(xnt, wts["u"], wts["vt"], s1, s2, e1, e2, tau, tile)
    return _final(h, peer, p2d, lw, wts["gate_w"], wts["ple_w"], final_g, tile)


def kernel(x_prompt, x_sample, state_shift, state_wkv, state_pool, p_prompt, p_sample, norm_mix_g, w_in, shift_mu, decay_w0, decay_b, a_0, a_b, g_b, k_k, k_a, r_k, lnx_g, lnx_b, pool_w, pool_scale, w_out, norm_ffn_g, peer_wq, peer_keys, peer_u, peer_v, norm_ple_g, ple_w, ple_gate_w, final_norm_g):
    assert norm_mix_g.shape[0] == 1, "single trunk layer"
    bsz, seq, _ = x_prompt.shape
    dbsz, dseq, _ = x_sample.shape
    row = lambda a: a.reshape(1, -1)
    lw = dict(
        shift_mu=row(shift_mu[0]), decay_w0=row(decay_w0[0]), decay_b=decay_b[0], a_0=row(a_0[0]), a_b=a_b[0],
        g_b=g_b[0], k_k=row(k_k[0]), k_a=row(k_a[0]), r_k=row(r_k[0]), lnx_g=row(lnx_g[0]), lnx_b=row(lnx_b[0]),
        pool_w=pool_w[0], pool_scale=row(pool_scale[0]), norm_ffn_g=row(norm_ffn_g[0]),
        norm_ple_g=row(norm_ple_g[0]))
    wts = dict(
        w_out=w_out[0].astype(BF16), wqt=peer_wq[0].T.astype(BF16), keys=peer_keys[0].astype(BF16),
        u=peer_u[0].astype(BF16), vt=peer_v[0].T.astype(BF16), gate_w=ple_gate_w[0].astype(BF16),
        ple_w=ple_w[0].astype(BF16))
    w_in_bf16 = w_in[0].astype(BF16)
    mix_g = row(norm_mix_g[0])
    final_g = row(final_norm_g)
    bd = _block_diag_ones()
    tile = 512

    xp = x_prompt.reshape(bsz * seq, D_MODEL)
    zp = _proj(xp, mix_g, w_in_bf16, tile)
    pm_tile = 256
    r, w, k, v, kk, b, g, bonus, yp = _premix(zp, lw, bd, pm_tile, seq // pm_tile)
    seq3 = lambda a: a.reshape(bsz, seq, D_RWKV)
    s0 = jnp.zeros((bsz, PAIRS, RWKV_HEAD, LANES), F32)
    o, s_fin = _scan(seq3(r), seq3(w), seq3(k), seq3(kk), seq3(b), seq3(v).transpose(0, 2, 1), s0, bsz, LANES)
    y_prompt = _tail(xp, p_prompt[0].reshape(bsz * seq, PLE_DIM), o.reshape(bsz * seq, D_RWKV), bonus, g, yp,
                     lw, wts, bd, final_g, tile).reshape(bsz, seq, D_MODEL)
    zp3 = zp.reshape(bsz, seq, D_IN)
    shift_prompt = zp3[:, -1, :D_SHIFT][None]
    wkv_prompt = _unpair_state(s_fin)[None]
    pool_prompt = zp3[:, seq - POOL_BUF:, D_SHIFT:][None]

    xs = x_sample.reshape(dbsz * dseq, D_MODEL)
    zs = _proj(xs, mix_g, w_in_bf16, dbsz * dseq).reshape(dbsz, dseq, D_IN)
    hist = jnp.zeros((dbsz, HALO, D_IN), F32)
    hist = hist.at[:, HALO - POOL_BUF:, D_SHIFT:].set(state_pool[0])
    hist = hist.at[:, HALO - 1, :D_SHIFT].set(state_shift[0])
    ext_len = HALO + dseq
    z_ext = jnp.concatenate([hist, zs], axis=1).reshape(dbsz * ext_len, D_IN)
    outs = _premix(z_ext, lw, bd, 512, 0)
    keep = lambda a: a.reshape(dbsz, ext_len, D_RWKV)[:, HALO:, :]
    r, w, k, v, kk, b, g, bonus, yp = [keep(a) for a in outs]
    o, s_fin = _scan(r, w, k, kk, b, v.transpose(0, 2, 1), _pair_state(state_wkv[0]), 8, dseq)
    flat = lambda a: a.reshape(dbsz * dseq, D_RWKV)
    y_sample = _tail(xs, p_sample[0].reshape(dbsz * dseq, PLE_DIM), flat(o), flat(bonus), flat(g), flat(yp),
                     lw, wts, bd, final_g, tile).reshape(dbsz, dseq, D_MODEL)
    shift_sample = zs[:, -1, :D_SHIFT][None]
    wkv_sample = _unpair_state(s_fin)[None]
    pool_sample = jnp.concatenate([state_pool[0], zs[:, :, D_SHIFT:]], axis=1)[:, -POOL_BUF:][None]

    return (y_prompt, y_sample, shift_prompt, wkv_prompt, pool_prompt, shift_sample, wkv_sample, pool_sample)
```

```python
import functools
import math

import jax
import jax.numpy as jnp
from jax import lax
from jax.experimental import pallas as pl
from jax.experimental.pallas import tpu as pltpu

F32 = jnp.float32
BF16 = jnp.bfloat16

D_MODEL = 1024
RWKV_HEAD = 64
D_RWKV = 512
N_RWKV_HEADS = 8
DECAY_LORA = 64
AAA_LORA = 64
GATE_LORA = 128
D_SHIFT = 3 * D_RWKV + DECAY_LORA + AAA_LORA + GATE_LORA
D_POOL = 512
POOL_WINDOWS = (2, 4, 8, 16)
POOL_GROUP = 128
POOL_BUF = 15
D_IN = D_SHIFT + D_POOL
N_KEYS = 128
N_EXPERTS = N_KEYS * N_KEYS
PEER_HEADS = 8
PEER_TOPK = 16
PEER_DK = 256
PEER_DK_HALF = 128
PLE_DIM = 256
NORM_EPS = 1e-6
LNX_EPS = 64e-5
PAST_LEN = 16384

HALO = 16
LANES = 128
SUBLANES = 8
VMEM_LIMIT = 48 * 1024 * 1024

NEG_INF = float("-inf")


def _cparams(*sem):
    return pltpu.CompilerParams(dimension_semantics=sem, vmem_limit_bytes=VMEM_LIMIT)


def _rms(x, g):
    return x * lax.rsqrt(jnp.mean(x * x, axis=-1, keepdims=True) + NORM_EPS) * g


def _sigmoid(x):
    return 1.0 / (1.0 + jnp.exp(-x))


def _bdot(a, b):
    return jnp.dot(a.astype(BF16), b.astype(BF16), preferred_element_type=F32)


def _seg_sum(x, bd):
    x1 = x.astype(BF16)
    r1 = x - x1.astype(F32)
    x2 = r1.astype(BF16)
    x3 = (r1 - x2.astype(F32)).astype(BF16)
    dot = functools.partial(jnp.dot, preferred_element_type=F32)
    return dot(x1, bd) + dot(x2, bd) + dot(x3, bd)


def _proj_body(x_ref, g_ref, w_ref, z_ref):
    z_ref[...] = _bdot(_rms(x_ref[...], g_ref[...]), w_ref[...])


def _proj(x2d, g, w_bf16, tile):
    rows = x2d.shape[0]
    return pl.pallas_call(
        _proj_body,
        grid=(rows // tile,),
        in_specs=[
            pl.BlockSpec((tile, D_MODEL), lambda i: (i, 0)),
            pl.BlockSpec((1, D_MODEL), lambda i: (0, 0)),
            pl.BlockSpec((D_MODEL, D_IN), lambda i: (0, 0)),
        ],
        out_specs=pl.BlockSpec((tile, D_IN), lambda i: (i, 0)),
        out_shape=jax.ShapeDtypeStruct((rows, D_IN), F32),
        compiler_params=_cparams("parallel"),
        name="proj",
    )(x2d, g, w_bf16)


def _premix_body(z_ref, halo_ref, mu_ref, w0_ref, decb_ref, a0_ref, ab_ref, gb_ref, kk_ref, ka_ref,
                 rk_ref, poolw_ref, pools_ref, bd_ref,
                 r_o, w_o, k_o, v_o, kkn_o, b_o, g_o, bonus_o, yp_o, ext_scr, *, tile, seq_tiles):
    halo = halo_ref[...]
    if seq_tiles:
        ti = pl.program_id(0) % seq_tiles
        halo = jnp.where(ti == 0, 0.0, halo)
    ext_scr[0:HALO, :] = halo
    ext_scr[HALO:HALO + tile, :] = z_ref[...]

    zr = z_ref[:, :D_SHIFT]
    zp = ext_scr[HALO - 1:HALO - 1 + tile, :D_SHIFT]
    zs = zr + (zp - zr) * mu_ref[...]
    r = zs[:, :D_RWKV]
    k = zs[:, D_RWKV:2 * D_RWKV]
    v = zs[:, 2 * D_RWKV:3 * D_RWKV]
    o = 3 * D_RWKV
    zw = zs[:, o:o + DECAY_LORA]
    za = zs[:, o + DECAY_LORA:o + DECAY_LORA + AAA_LORA]
    zg = zs[:, o + DECAY_LORA + AAA_LORA:]
    dec_in = w0_ref[...] + _bdot(jnp.tanh(zw), decb_ref[...])
    nx = -dec_in
    softplus = jnp.maximum(nx, 0.0) + jnp.log1p(jnp.exp(-jnp.abs(nx)))
    decay = jnp.exp(-jnp.exp(-softplus - 0.5))
    a = _sigmoid(a0_ref[...] + _bdot(za, ab_ref[...]))
    g = _bdot(_sigmoid(zg), gb_ref[...])
    bd = bd_ref[...]
    kkf = k * kk_ref[...]
    kkn = kkf / jnp.maximum(jnp.sqrt(_seg_sum(kkf * kkf, bd)), 1e-12)
    k2 = k * (1.0 + (a - 1.0) * ka_ref[...])
    bonus = _seg_sum(r * k2 * rk_ref[...], bd) * v
    r_o[...] = r
    w_o[...] = decay
    k_o[...] = k2
    v_o[...] = v
    kkn_o[...] = kkn
    b_o[...] = kkn * a
    g_o[...] = g
    bonus_o[...] = bonus

    if seq_tiles:
        pos = ti * tile + lax.broadcasted_iota(jnp.int32, (tile, 1), 0)
    for gi, wdw in enumerate(POOL_WINDOWS):
        cols = slice(D_SHIFT + gi * POOL_GROUP, D_SHIFT + (gi + 1) * POOL_GROUP)
        u_ext = ext_scr[:, cols]
        s = u_ext
        shift = 1
        while shift < wdw:
            s = s + pltpu.roll(s, shift, 0)
            shift *= 2
        u = u_ext[HALO:, :]
        if seq_tiles:
            cnt = jnp.minimum(pos + 1, wdw).astype(F32)
        else:
            cnt = float(wdw)
        pooled = s[HALO:, :] / cnt - u
        y = _bdot(pooled, poolw_ref[gi])
        yp_o[:, gi * POOL_GROUP:(gi + 1) * POOL_GROUP] = y * pools_ref[:, gi * POOL_GROUP:(gi + 1) * POOL_GROUP]


def _premix(z, lw, bd, tile, seq_tiles):
    rows = z.shape[0]
    body = functools.partial(_premix_body, tile=tile, seq_tiles=seq_tiles)
    full = lambda shape: pl.BlockSpec(shape, lambda i: (0,) * len(shape))
    out_block = pl.BlockSpec((tile, D_RWKV), lambda i: (i, 0))
    out = jax.ShapeDtypeStruct((rows, D_RWKV), F32)
    halo_blocks = tile // HALO
    return pl.pallas_call(
        body,
        grid=(rows // tile,),
        in_specs=[
            pl.BlockSpec((tile, D_IN), lambda i: (i, 0)),
            pl.BlockSpec((HALO, D_IN), lambda i: (jnp.maximum(i * halo_blocks - 1, 0), 0)),
            full((1, D_SHIFT)), full((1, D_RWKV)), full((DECAY_LORA, D_RWKV)), full((1, D_RWKV)),
            full((AAA_LORA, D_RWKV)), full((GATE_LORA, D_RWKV)), full((1, D_RWKV)), full((1, D_RWKV)),
            full((1, D_RWKV)), full((4, POOL_GROUP, POOL_GROUP)), full((1, D_POOL)), full((D_RWKV, D_RWKV)),
        ],
        out_specs=[out_block] * 9,
        out_shape=[out] * 9,
        scratch_shapes=[pltpu.VMEM((tile + HALO, D_IN), F32)],
        compiler_params=_cparams("parallel"),
        name="premix",
    )(z, z, lw["shift_mu"], lw["decay_w0"], lw["decay_b"], lw["a_0"], lw["a_b"], lw["g_b"], lw["k_k"],
      lw["k_a"], lw["r_k"], lw["pool_w"], lw["pool_scale"], bd)


GROUP_SEQS = LANES // (2 * N_RWKV_HEADS)
HALF_ROWS = RWKV_HEAD // 2
N_ACC = 4


def _tree_sum(parts):
    while len(parts) > 1:
        parts = [a + b for a, b in zip(parts[::2], parts[1::2])]
    return parts[0]


def _scan_body(w_ref, kk_ref, b_ref, k_ref, r_ref, v_ref, s0_ref, o_ref, sout_ref, st_scr, *, tc):
    c = pl.program_id(1)

    @pl.when(c == 0)
    def _():
        st_scr[...] = s0_ref[0]

    def step(t, carry):
        acc = [None] * N_ACC
        for j in range(RWKV_HEAD):
            term = st_scr[j] * kk_ref[0, t, pl.ds(j, 1), :]
            acc[j % N_ACC] = term if acc[j % N_ACC] is None else acc[j % N_ACC] + term
        sk = _tree_sum(acc)
        v = v_ref[0, t]
        acc = [None] * N_ACC
        for j in range(RWKV_HEAD):
            row_j = pl.ds(j, 1)
            s = (st_scr[j] * w_ref[0, t, row_j, :] - sk * b_ref[0, t, row_j, :]) + v * k_ref[0, t, row_j, :]
            st_scr[j] = s
            term = s * r_ref[0, t, row_j, :]
            acc[j % N_ACC] = term if acc[j % N_ACC] is None else acc[j % N_ACC] + term
        o_ref[0, t] = _tree_sum(acc)
        return carry

    if tc < SUBLANES:
        for t in range(tc):
            step(t, 0)
    else:
        lax.fori_loop(0, tc, step, 0)

    @pl.when(c == pl.num_programs(1) - 1)
    def _():
        sout_ref[0] = st_scr[...]


def _keys_to_lanes(x):
    bsz, t_len, _ = x.shape
    y = x.reshape(bsz // GROUP_SEQS, GROUP_SEQS, t_len, N_RWKV_HEADS, RWKV_HEAD).transpose(0, 2, 4, 1, 3)
    y = y.reshape(bsz // GROUP_SEQS, t_len, RWKV_HEAD, LANES // 2)
    return jnp.concatenate([y, y], axis=-1)


def _values_to_lanes(x):
    bsz, t_len, _ = x.shape
    y = x.reshape(bsz // GROUP_SEQS, GROUP_SEQS, t_len, N_RWKV_HEADS, 2, HALF_ROWS).transpose(0, 2, 5, 4, 1, 3)
    return y.reshape(bsz // GROUP_SEQS, t_len, HALF_ROWS, LANES)


def _values_from_lanes(y):
    groups, t_len = y.shape[:2]
    y = y.reshape(groups, t_len, HALF_ROWS, 2, GROUP_SEQS, N_RWKV_HEADS).transpose(0, 4, 1, 5, 3, 2)
    return y.reshape(groups * GROUP_SEQS, t_len, D_RWKV)


def _state_to_lanes(s):
    bsz = s.shape[0]
    y = s.reshape(bsz // GROUP_SEQS, GROUP_SEQS, N_RWKV_HEADS, 2, HALF_ROWS, RWKV_HEAD).transpose(0, 5, 4, 3, 1, 2)
    return y.reshape(bsz // GROUP_SEQS, RWKV_HEAD, HALF_ROWS, LANES)


def _state_from_lanes(y):
    groups = y.shape[0]
    y = y.reshape(groups, RWKV_HEAD, HALF_ROWS, 2, GROUP_SEQS, N_RWKV_HEADS).transpose(0, 4, 5, 3, 2, 1)
    return y.reshape(groups * GROUP_SEQS, N_RWKV_HEADS, RWKV_HEAD, RWKV_HEAD)


def _scan(r, w, k, v, kk, b, s0, tc):
    bsz, t_len, _ = r.shape
    groups = bsz // GROUP_SEQS
    body = functools.partial(_scan_body, tc=tc)
    keys = pl.BlockSpec((1, tc, RWKV_HEAD, LANES), lambda i, c: (i, c, 0, 0))
    vals = pl.BlockSpec((1, tc, HALF_ROWS, LANES), lambda i, c: (i, c, 0, 0))
    st = pl.BlockSpec((1, RWKV_HEAD, HALF_ROWS, LANES), lambda i, c: (i, 0, 0, 0))
    o, s_fin = pl.pallas_call(
        body,
        grid=(groups, t_len // tc),
        in_specs=[keys, keys, keys, keys, keys, vals, st],
        out_specs=[vals, st],
        out_shape=[jax.ShapeDtypeStruct((groups, t_len, HALF_ROWS, LANES), F32),
                   jax.ShapeDtypeStruct((groups, RWKV_HEAD, HALF_ROWS, LANES), F32)],
        scratch_shapes=[pltpu.VMEM((RWKV_HEAD, HALF_ROWS, LANES), F32)],
        compiler_params=_cparams("parallel", "arbitrary"),
        name="scan",
    )(_keys_to_lanes(w), _keys_to_lanes(kk), _keys_to_lanes(b), _keys_to_lanes(k), _keys_to_lanes(r),
      _values_to_lanes(v), _state_to_lanes(s0))
    return _values_from_lanes(o), _state_from_lanes(s_fin)


def _postmix_body(x_ref, o_ref, bonus_ref, g_ref, yp_ref, lng_ref, lnb_ref, wout_ref, nfg_ref, bd_ref,
                  h_o, xnt_o):
    bd = bd_ref[...]
    o = o_ref[...]
    inv_n = 1.0 / RWKV_HEAD
    d = o - _seg_sum(o, bd) * inv_n
    var = _seg_sum(d * d, bd) * inv_n
    y_r = (d * lax.rsqrt(var + LNX_EPS) * lng_ref[...] + lnb_ref[...] + bonus_ref[...]) * g_ref[...]
    h = x_ref[...] + _bdot(y_r, wout_ref[:D_RWKV, :]) + _bdot(yp_ref[...], wout_ref[D_RWKV:, :])
    h_o[...] = h
    xnt_o[...] = _rms(h, nfg_ref[...]).T.astype(BF16)


def _postmix(x2d, o, bonus, g, yp, lw, w_out_bf16, bd, tile):
    rows = x2d.shape[0]
    full = lambda shape: pl.BlockSpec(shape, lambda i: (0,) * len(shape))
    half = pl.BlockSpec((tile, D_RWKV), lambda i: (i, 0))
    wide = pl.BlockSpec((tile, D_MODEL), lambda i: (i, 0))
    return pl.pallas_call(
        _postmix_body,
        grid=(rows // tile,),
        in_specs=[wide, half, half, half, half, full((1, D_RWKV)), full((1, D_RWKV)),
                  full((D_MODEL, D_MODEL)), full((1, D_MODEL)), full((D_RWKV, D_RWKV))],
        out_specs=[wide, pl.BlockSpec((D_MODEL, tile), lambda i: (0, i))],
        out_shape=[jax.ShapeDtypeStruct((rows, D_MODEL), F32), jax.ShapeDtypeStruct((D_MODEL, rows), BF16)],
        compiler_params=_cparams("parallel"),
        name="postmix",
    )(x2d, o, bonus, g, yp, lw["lnx_g"], lw["lnx_b"], w_out_bf16, lw["norm_ffn_g"], bd)


def _oddeven_mergesort_pairs(n):
    pairs = []
    p = 1
    while p < n:
        k = p
        while k >= 1:
            for j in range(k % p, n - k, 2 * k):
                for i in range(min(k, n - j - k)):
                    if (i + j) // (2 * p) == (i + j + k) // (2 * p):
                        pairs.append((i + j, i + j + k))
            k //= 2
        p *= 2
    return pairs


_SORT16 = _oddeven_mergesort_pairs(PEER_TOPK)


def _exchange(lst, i, j):
    lst[i], lst[j] = jnp.maximum(lst[i], lst[j]), jnp.minimum(lst[i], lst[j])


def _merge_top16(a, b):
    c = [jnp.maximum(a[i], b[PEER_TOPK - 1 - i]) for i in range(PEER_TOPK)]
    stride = PEER_TOPK // 2
    while stride:
        for i in range(PEER_TOPK):
            if not i & stride:
                _exchange(c, i, i + stride)
        stride //= 2
    return c


def _top16_of_keys(s):
    lst = [s[SUBLANES * g:SUBLANES * (g + 1), :] for g in range(N_KEYS // SUBLANES)]
    for i, j in _SORT16:
        _exchange(lst, i, j)
    shift = SUBLANES // 2
    while shift:
        lst = _merge_top16(lst, [pltpu.roll(x, shift, 0) for x in lst])
        shift //= 2
    return lst


def _route_body(xnt_ref, wqt_ref, keys_ref, s1_o, s2_o, e1_o, e2_o, tau_o, *, tile):
    qt = jnp.dot(wqt_ref[...], xnt_ref[...], preferred_element_type=F32).astype(BF16)
    head = lax.broadcasted_iota(jnp.int32, (PEER_HEADS, tile), 0)
    neg = jnp.full((PEER_HEADS, tile), NEG_INF, F32)
    top = [[neg] * PEER_TOPK, [neg] * PEER_TOPK]
    for h in range(PEER_HEADS):
        base = h * PEER_DK
        for half, s_o in enumerate((s1_o, s2_o)):
            q_half = qt[base + half * PEER_DK_HALF:base + (half + 1) * PEER_DK_HALF]
            s = jnp.dot(keys_ref[half], q_half, preferred_element_type=F32)
            s_o[h] = s
            best = _top16_of_keys(s)
            top[half] = [jnp.where(head == h, best[a], top[half][a]) for a in range(PEER_TOPK)]
    v1, v2 = top
    pad = lambda lst: lst + [neg] * (PEER_TOPK - len(lst))
    sc = [v1[0] + v2[b] for b in range(PEER_TOPK)]
    for a in range(1, PEER_TOPK // 2):
        sc = _merge_top16(sc, pad([v1[a] + v2[b] for b in range(PEER_TOPK // (a + 1))]))
    sc = _merge_top16(sc, pad([v1[a] + v2[0] for a in range(PEER_TOPK // 2, PEER_TOPK)]))
    z = jnp.ones_like(sc[0])
    for kth in sc[1:]:
        z = z + jnp.exp(kth - sc[0])
    tau_o[...] = sc[PEER_TOPK - 1]
    for h in range(PEER_HEADS):
        row = slice(h, h + 1)
        e1_o[h] = jnp.exp(s1_o[h] - v1[0][row])
        e2_o[h] = jnp.exp(s2_o[h] - v2[0][row]) / z[row]


def _route(xnt, wqt_bf16, keys_bf16, tile):
    rows = xnt.shape[1]
    body = functools.partial(_route_body, tile=tile)
    sco = pl.BlockSpec((PEER_HEADS, N_KEYS, tile), lambda i: (0, 0, i))
    sc_shape = jax.ShapeDtypeStruct((PEER_HEADS, N_KEYS, rows), F32)
    return pl.pallas_call(
        body,
        grid=(rows // tile,),
        in_specs=[
            pl.BlockSpec((D_MODEL, tile), lambda i: (0, i)),
            pl.BlockSpec((PEER_HEADS * PEER_DK, D_MODEL), lambda i: (0, 0)),
            pl.BlockSpec((2, N_KEYS, PEER_DK_HALF), lambda i: (0, 0, 0)),
        ],
        out_specs=[sco, sco, sco, sco, pl.BlockSpec((PEER_HEADS, tile), lambda i: (0, i))],
        out_shape=[sc_shape, sc_shape, sc_shape, sc_shape, jax.ShapeDtypeStruct((PEER_HEADS, rows), F32)],
        compiler_params=_cparams("parallel"),
        name="route",
    )(xnt, wqt_bf16, keys_bf16)


E1_PER_CHUNK = 8
CHUNK = E1_PER_CHUNK * N_KEYS
PIECE_ROWS = 32


def _gelu(x):
    return 0.5 * x * (1.0 + lax.erf(x * (1.0 / math.sqrt(2.0))))


def _peer_body(xnt_ref, u_ref, vt_ref, s1_ref, s2_ref, e1_ref, e2_ref, tau_ref, out_o, acc_scr, ht_scr, a_scr,
               bc_scr, *, tile):
    c = pl.program_id(1)

    @pl.when(c == 0)
    def _():
        acc_scr[...] = jnp.zeros_like(acc_scr)

    ht_scr[...] = _gelu(jnp.dot(u_ref[...], xnt_ref[...], preferred_element_type=F32))
    chunk_e1 = pl.ds(pl.multiple_of(c * E1_PER_CHUNK, E1_PER_CHUNK), E1_PER_CHUNK)
    for h in range(PEER_HEADS):
        s1_rows = s1_ref[h, chunk_e1, :]
        e1_rows = e1_ref[h, chunk_e1, :]
        for e in range(E1_PER_CHUNK):
            bc_scr[0, h, e] = jnp.broadcast_to(s1_rows[e:e + 1], (SUBLANES, tile))
            bc_scr[1, h, e] = jnp.broadcast_to(e1_rows[e:e + 1], (SUBLANES, tile))
    n_sub = PIECE_ROWS // SUBLANES
    n_rp = N_KEYS // PIECE_ROWS

    def piece(i, carry):
        lanes = pl.ds(pl.multiple_of((i // n_rp) * LANES, LANES), LANES)
        row0 = (i % n_rp) * PIECE_ROWS
        sub_rows = [pl.ds(pl.multiple_of(row0 + q * SUBLANES, SUBLANES), SUBLANES) for q in range(n_sub)]
        wgt = [[None] * n_sub for _ in range(E1_PER_CHUNK)]
        for h in range(PEER_HEADS):
            tau = tau_ref[pl.ds(h, 1), lanes]
            s2 = [s2_ref[h, rows, lanes] for rows in sub_rows]
            e2 = [e2_ref[h, rows, lanes] for rows in sub_rows]
            for e in range(E1_PER_CHUNK):
                s1 = bc_scr[0, h, e, :, lanes]
                e1 = bc_scr[1, h, e, :, lanes]
                for q in range(n_sub):
                    term = jnp.where((s1 + s2[q]) >= tau, e1 * e2[q], 0.0)
                    wgt[e][q] = term if wgt[e][q] is None else wgt[e][q] + term
        for e in range(E1_PER_CHUNK):
            for q in range(0, n_sub, 2):
                rows = pl.ds(pl.multiple_of(e * N_KEYS + row0 + q * SUBLANES, 2 * SUBLANES), 2 * SUBLANES)
                gate = jnp.concatenate([wgt[e][q], wgt[e][q + 1]], axis=0)
                a_scr[rows, lanes] = (ht_scr[rows, lanes] * gate).astype(BF16)
        return carry

    lax.fori_loop(0, (tile // LANES) * n_rp, piece, 0)
    acc_scr[...] += jnp.dot(vt_ref[...], a_scr[...], preferred_element_type=F32)

    @pl.when(c == pl.num_programs(1) - 1)
    def _():
        out_o[...] = acc_scr[...].T


def _peer(xnt, u_bf16, vt_bf16, s1, s2, e1, e2, tau, tile):
    rows = xnt.shape[1]
    body = functools.partial(_peer_body, tile=tile)
    sco = pl.BlockSpec((PEER_HEADS, N_KEYS, tile), lambda i, c: (0, 0, i))
    return pl.pallas_call(
        body,
        grid=(rows // tile, N_EXPERTS // CHUNK),
        in_specs=[
            pl.BlockSpec((D_MODEL, tile), lambda i, c: (0, i)),
            pl.BlockSpec((CHUNK, D_MODEL), lambda i, c: (c, 0)),
            pl.BlockSpec((D_MODEL, CHUNK), lambda i, c: (0, c)),
            sco, sco, sco, sco,
            pl.BlockSpec((PEER_HEADS, tile), lambda i, c: (0, i)),
        ],
        out_specs=pl.BlockSpec((tile, D_MODEL), lambda i, c: (i, 0)),
        out_shape=jax.ShapeDtypeStruct((rows, D_MODEL), F32),
        scratch_shapes=[pltpu.VMEM((D_MODEL, tile), F32), pltpu.VMEM((CHUNK, tile), F32),
                        pltpu.VMEM((CHUNK, tile), BF16),
                        pltpu.VMEM((2, PEER_HEADS, E1_PER_CHUNK, SUBLANES, tile), F32)],
        compiler_params=_cparams("parallel", "arbitrary"),
        name="peer",
    )(xnt, u_bf16, vt_bf16, s1, s2, e1, e2, tau)


def _final_body(h_ref, peer_ref, p_ref, npg_ref, gatew_ref, plew_ref, fg_ref, y_o):
    h = h_ref[...] + peer_ref[...]
    gate = _sigmoid(_bdot(_rms(h, npg_ref[...]), gatew_ref[...]))
    h = h + _bdot(p_ref[...], plew_ref[...]) * gate
    y_o[...] = _rms(h, fg_ref[...])


def _final(h, peer, p2d, lw, gate_w_bf16, ple_w_bf16, final_g, tile):
    rows = h.shape[0]
    full = lambda shape: pl.BlockSpec(shape, lambda i: (0,) * len(shape))
    wide = pl.BlockSpec((tile, D_MODEL), lambda i: (i, 0))
    return pl.pallas_call(
        _final_body,
        grid=(rows // tile,),
        in_specs=[wide, wide, pl.BlockSpec((tile, PLE_DIM), lambda i: (i, 0)), full((1, D_MODEL)),
                  full((D_MODEL, D_MODEL)), full((PLE_DIM, D_MODEL)), full((1, D_MODEL))],
        out_specs=wide,
        out_shape=jax.ShapeDtypeStruct((rows, D_MODEL), F32),
        compiler_params=_cparams("parallel"),
        name="final",
    )(h, peer, p2d, lw["norm_ple_g"], gate_w_bf16, ple_w_bf16, final_g)


def _block_diag_ones():
    seg = jnp.arange(D_RWKV) // RWKV_HEAD
    return (seg[:, None] == seg[None, :]).astype(BF16)


def _tail(x2d, p2d, o, bonus, g, yp, lw, wts, bd, final_g, tile):
    h, xnt = _postmix(x2d, o, bonus, g, yp, lw, wts["w_out"], bd, tile)
    s1, s2, e1, e2, tau = _route(xnt, wts["wqt"], wts["keys"], tile)
    peer = _peer(xnt, wts["u"], wts["vt"], s1, s2, e1, e2, tau, tile)
    return _final(h, peer, p2d, lw, wts["gate_w"], wts["ple_w"], final_g, tile)


def kernel(x_prompt, x_sample, state_shift, state_wkv, state_pool, p_prompt, p_sample, norm_mix_g, w_in, shift_mu, decay_w0, decay_b, a_0, a_b, g_b, k_k, k_a, r_k, lnx_g, lnx_b, pool_w, pool_scale, w_out, norm_ffn_g, peer_wq, peer_keys, peer_u, peer_v, norm_ple_g, ple_w, ple_gate_w, final_norm_g):
    assert norm_mix_g.shape[0] == 1, "single trunk layer"
    bsz, seq, _ = x_prompt.shape
    dbsz, dseq, _ = x_sample.shape
    row = lambda a: a.reshape(1, -1)
    lw = dict(
        shift_mu=row(shift_mu[0]), decay_w0=row(decay_w0[0]), decay_b=decay_b[0], a_0=row(a_0[0]), a_b=a_b[0],
        g_b=g_b[0], k_k=row(k_k[0]), k_a=row(k_a[0]), r_k=row(r_k[0]), lnx_g=row(lnx_g[0]), lnx_b=row(lnx_b[0]),
        pool_w=pool_w[0], pool_scale=row(pool_scale[0]), norm_ffn_g=row(norm_ffn_g[0]),
        norm_ple_g=row(norm_ple_g[0]))
    wts = dict(
        w_out=w_out[0].astype(BF16), wqt=peer_wq[0].T.astype(BF16), keys=peer_keys[0].astype(BF16),
        u=peer_u[0].astype(BF16), vt=peer_v[0].T.astype(BF16), gate_w=ple_gate_w[0].astype(BF16),
        ple_w=ple_w[0].astype(BF16))
    w_in_bf16 = w_in[0].astype(BF16)
    mix_g = row(norm_mix_g[0])
    final_g = row(final_norm_g)
    bd = _block_diag_ones()
    tile = 512

    xp = x_prompt.reshape(bsz * seq, D_MODEL)
    zp = _proj(xp, mix_g, w_in_bf16, tile)
    pm_tile = 256
    r, w, k, v, kk, b, g, bonus, yp = _premix(zp, lw, bd, pm_tile, seq // pm_tile)
    seq3 = lambda a: a.reshape(bsz, seq, D_RWKV)
    s0 = jnp.zeros((bsz, N_RWKV_HEADS, RWKV_HEAD, RWKV_HEAD), F32)
    o, s_fin = _scan(seq3(r), seq3(w), seq3(k), seq3(v), seq3(kk), seq3(b), s0, 64)
    y_prompt = _tail(xp, p_prompt[0].reshape(bsz * seq, PLE_DIM), o.reshape(bsz * seq, D_RWKV), bonus, g, yp,
                     lw, wts, bd, final_g, tile).reshape(bsz, seq, D_MODEL)
    zp3 = zp.reshape(bsz, seq, D_IN)
    shift_prompt = zp3[:, -1, :D_SHIFT][None]
    wkv_prompt = s_fin[None]
    pool_prompt = zp3[:, seq - POOL_BUF:, D_SHIFT:][None]

    xs = x_sample.reshape(dbsz * dseq, D_MODEL)
    zs = _proj(xs, mix_g, w_in_bf16, dbsz * dseq).reshape(dbsz, dseq, D_IN)
    hist = jnp.zeros((dbsz, HALO, D_IN), F32)
    hist = hist.at[:, HALO - POOL_BUF:, D_SHIFT:].set(state_pool[0])
    hist = hist.at[:, HALO - 1, :D_SHIFT].set(state_shift[0])
    ext_len = HALO + dseq
    z_ext = jnp.concatenate([hist, zs], axis=1).reshape(dbsz * ext_len, D_IN)
    outs = _premix(z_ext, lw, bd, 512, 0)
    keep = lambda a: a.reshape(dbsz, ext_len, D_RWKV)[:, HALO:, :]
    r, w, k, v, kk, b, g, bonus, yp = [keep(a) for a in outs]
    o, s_fin = _scan(r, w, k, v, kk, b, state_wkv[0], dseq)
    flat = lambda a: a.reshape(dbsz * dseq, D_RWKV)
    y_sample = _tail(xs, p_sample[0].reshape(dbsz * dseq, PLE_DIM), flat(o), flat(bonus), flat(g), flat(yp),
                     lw, wts, bd, final_g, tile).reshape(dbsz, dseq, D_MODEL)
    shift_sample = zs[:, -1, :D_SHIFT][None]
    wkv_sample = s_fin[None]
    pool_sample = jnp.concatenate([state_pool[0], zs[:, :, D_SHIFT:]], axis=1)[:, -POOL_BUF:][None]

    return (y_prompt, y_sample, shift_prompt, wkv_prompt, pool_prompt, shift_sample, wkv_sample, pool_sample)
```

```python
import functools
import math

import jax
import jax.numpy as jnp
from jax import lax
from jax.experimental import pallas as pl
from jax.experimental.pallas import tpu as pltpu

F32 = jnp.float32
BF16 = jnp.bfloat16

D_MODEL = 1024
RWKV_HEAD = 64
D_RWKV = 512
N_RWKV_HEADS = 8
DECAY_LORA = 64
AAA_LORA = 64
GATE_LORA = 128
D_SHIFT = 3 * D_RWKV + DECAY_LORA + AAA_LORA + GATE_LORA
D_POOL = 512
POOL_WINDOWS = (2, 4, 8, 16)
POOL_GROUP = 128
POOL_BUF = 15
D_IN = D_SHIFT + D_POOL
N_KEYS = 128
N_EXPERTS = N_KEYS * N_KEYS
PEER_HEADS = 8
PEER_TOPK = 16
PEER_DK = 256
PEER_DK_HALF = 128
PLE_DIM = 256
NORM_EPS = 1e-6
LNX_EPS = 64e-5
PAST_LEN = 16384

HALO = 16
LANES = 128
SUBLANES = 8
VMEM_LIMIT = 48 * 1024 * 1024

NEG_INF = float("-inf")


def _cparams(*sem):
    return pltpu.CompilerParams(dimension_semantics=sem, vmem_limit_bytes=VMEM_LIMIT)


def _rms(x, g):
    return x * lax.rsqrt(jnp.mean(x * x, axis=-1, keepdims=True) + NORM_EPS) * g


def _sigmoid(x):
    return 1.0 / (1.0 + jnp.exp(-x))


def _bdot(a, b):
    return jnp.dot(a.astype(BF16), b.astype(BF16), preferred_element_type=F32)


def _seg_sum(x, bd):
    x1 = x.astype(BF16)
    r1 = x - x1.astype(F32)
    x2 = r1.astype(BF16)
    x3 = (r1 - x2.astype(F32)).astype(BF16)
    dot = functools.partial(jnp.dot, preferred_element_type=F32)
    return dot(x1, bd) + dot(x2, bd) + dot(x3, bd)


def _proj_body(x_ref, g_ref, w_ref, z_ref):
    z_ref[...] = _bdot(_rms(x_ref[...], g_ref[...]), w_ref[...])


def _proj(x2d, g, w_bf16, tile):
    rows = x2d.shape[0]
    return pl.pallas_call(
        _proj_body,
        grid=(rows // tile,),
        in_specs=[
            pl.BlockSpec((tile, D_MODEL), lambda i: (i, 0)),
            pl.BlockSpec((1, D_MODEL), lambda i: (0, 0)),
            pl.BlockSpec((D_MODEL, D_IN), lambda i: (0, 0)),
        ],
        out_specs=pl.BlockSpec((tile, D_IN), lambda i: (i, 0)),
        out_shape=jax.ShapeDtypeStruct((rows, D_IN), F32),
        compiler_params=_cparams("parallel"),
        name="proj",
    )(x2d, g, w_bf16)


def _premix_body(z_ref, halo_ref, mu_ref, w0_ref, decb_ref, a0_ref, ab_ref, gb_ref, kk_ref, ka_ref,
                 rk_ref, poolw_ref, pools_ref, bd_ref,
                 r_o, w_o, k_o, v_o, kkn_o, b_o, g_o, bonus_o, yp_o, ext_scr, *, tile, seq_tiles):
    halo = halo_ref[...]
    if seq_tiles:
        ti = pl.program_id(0) % seq_tiles
        halo = jnp.where(ti == 0, 0.0, halo)
    ext_scr[0:HALO, :] = halo
    ext_scr[HALO:HALO + tile, :] = z_ref[...]

    zr = z_ref[:, :D_SHIFT]
    zp = ext_scr[HALO - 1:HALO - 1 + tile, :D_SHIFT]
    zs = zr + (zp - zr) * mu_ref[...]
    r = zs[:, :D_RWKV]
    k = zs[:, D_RWKV:2 * D_RWKV]
    v = zs[:, 2 * D_RWKV:3 * D_RWKV]
    o = 3 * D_RWKV
    zw = zs[:, o:o + DECAY_LORA]
    za = zs[:, o + DECAY_LORA:o + DECAY_LORA + AAA_LORA]
    zg = zs[:, o + DECAY_LORA + AAA_LORA:]
    dec_in = w0_ref[...] + _bdot(jnp.tanh(zw), decb_ref[...])
    nx = -dec_in
    softplus = jnp.maximum(nx, 0.0) + jnp.log1p(jnp.exp(-jnp.abs(nx)))
    decay = jnp.exp(-jnp.exp(-softplus - 0.5))
    a = _sigmoid(a0_ref[...] + _bdot(za, ab_ref[...]))
    g = _bdot(_sigmoid(zg), gb_ref[...])
    bd = bd_ref[...]
    kkf = k * kk_ref[...]
    kkn = kkf / jnp.maximum(jnp.sqrt(_seg_sum(kkf * kkf, bd)), 1e-12)
    k2 = k * (1.0 + (a - 1.0) * ka_ref[...])
    bonus = _seg_sum(r * k2 * rk_ref[...], bd) * v
    r_o[...] = r
    w_o[...] = decay
    k_o[...] = k2
    v_o[...] = v
    kkn_o[...] = kkn
    b_o[...] = kkn * a
    g_o[...] = g
    bonus_o[...] = bonus

    if seq_tiles:
        pos = ti * tile + lax.broadcasted_iota(jnp.int32, (tile, 1), 0)
    for gi, wdw in enumerate(POOL_WINDOWS):
        cols = slice(D_SHIFT + gi * POOL_GROUP, D_SHIFT + (gi + 1) * POOL_GROUP)
        u_ext = ext_scr[:, cols]
        s = u_ext
        shift = 1
        while shift < wdw:
            s = s + pltpu.roll(s, shift, 0)
            shift *= 2
        u = u_ext[HALO:, :]
        if seq_tiles:
            cnt = jnp.minimum(pos + 1, wdw).astype(F32)
        else:
            cnt = float(wdw)
        pooled = s[HALO:, :] / cnt - u
        y = _bdot(pooled, poolw_ref[gi])
        yp_o[:, gi * POOL_GROUP:(gi + 1) * POOL_GROUP] = y * pools_ref[:, gi * POOL_GROUP:(gi + 1) * POOL_GROUP]


def _premix(z, lw, bd, tile, seq_tiles):
    rows = z.shape[0]
    body = functools.partial(_premix_body, tile=tile, seq_tiles=seq_tiles)
    full = lambda shape: pl.BlockSpec(shape, lambda i: (0,) * len(shape))
    out_block = pl.BlockSpec((tile, D_RWKV), lambda i: (i, 0))
    out = jax.ShapeDtypeStruct((rows, D_RWKV), F32)
    halo_blocks = tile // HALO
    return pl.pallas_call(
        body,
        grid=(rows // tile,),
        in_specs=[
            pl.BlockSpec((tile, D_IN), lambda i: (i, 0)),
            pl.BlockSpec((HALO, D_IN), lambda i: (jnp.maximum(i * halo_blocks - 1, 0), 0)),
            full((1, D_SHIFT)), full((1, D_RWKV)), full((DECAY_LORA, D_RWKV)), full((1, D_RWKV)),
            full((AAA_LORA, D_RWKV)), full((GATE_LORA, D_RWKV)), full((1, D_RWKV)), full((1, D_RWKV)),
            full((1, D_RWKV)), full((4, POOL_GROUP, POOL_GROUP)), full((1, D_POOL)), full((D_RWKV, D_RWKV)),
        ],
        out_specs=[out_block] * 9,
        out_shape=[out] * 9,
        scratch_shapes=[pltpu.VMEM((tile + HALO, D_IN), F32)],
        compiler_params=_cparams("parallel"),
        name="premix",
    )(z, z, lw["shift_mu"], lw["decay_w0"], lw["decay_b"], lw["a_0"], lw["a_b"], lw["g_b"], lw["k_k"],
      lw["k_a"], lw["r_k"], lw["pool_w"], lw["pool_scale"], bd)


GROUP_SEQS = LANES // (2 * N_RWKV_HEADS)
HALF_ROWS = RWKV_HEAD // 2
N_ACC = 4


def _tree_sum(parts):
    while len(parts) > 1:
        parts = [a + b for a, b in zip(parts[::2], parts[1::2])]
    return parts[0]


def _scan_body(w_ref, kk_ref, b_ref, k_ref, r_ref, v_ref, s0_ref, o_ref, sout_ref, st_scr, *, tc):
    c = pl.program_id(1)

    @pl.when(c == 0)
    def _():
        st_scr[...] = s0_ref[0]

    def step(t, carry):
        acc = [None] * N_ACC
        for j in range(RWKV_HEAD):
            term = st_scr[j] * kk_ref[0, t, pl.ds(j, 1), :]
            acc[j % N_ACC] = term if acc[j % N_ACC] is None else acc[j % N_ACC] + term
        sk = _tree_sum(acc)
        v = v_ref[0, t]
        acc = [None] * N_ACC
        for j in range(RWKV_HEAD):
            row_j = pl.ds(j, 1)
            s = (st_scr[j] * w_ref[0, t, row_j, :] - sk * b_ref[0, t, row_j, :]) + v * k_ref[0, t, row_j, :]
            st_scr[j] = s
            term = s * r_ref[0, t, row_j, :]
            acc[j % N_ACC] = term if acc[j % N_ACC] is None else acc[j % N_ACC] + term
        o_ref[0, t] = _tree_sum(acc)
        return carry

    if tc < SUBLANES:
        for t in range(tc):
            step(t, 0)
    else:
        lax.fori_loop(0, tc, step, 0)

    @pl.when(c == pl.num_programs(1) - 1)
    def _():
        sout_ref[0] = st_scr[...]


def _keys_to_lanes(x):
    bsz, t_len, _ = x.shape
    y = x.reshape(bsz // GROUP_SEQS, 1, GROUP_SEQS, t_len, N_RWKV_HEADS, RWKV_HEAD)
    y = jnp.broadcast_to(y, (bsz // GROUP_SEQS, 2, GROUP_SEQS, t_len, N_RWKV_HEADS, RWKV_HEAD))
    return y.transpose(0, 3, 5, 1, 2, 4).reshape(bsz // GROUP_SEQS, t_len, RWKV_HEAD, LANES)


def _values_to_lanes(x):
    bsz, t_len, _ = x.shape
    y = x.reshape(bsz // GROUP_SEQS, GROUP_SEQS, t_len, N_RWKV_HEADS, 2, HALF_ROWS).transpose(0, 2, 5, 4, 1, 3)
    return y.reshape(bsz // GROUP_SEQS, t_len, HALF_ROWS, LANES)


def _values_from_lanes(y):
    groups, t_len = y.shape[:2]
    y = y.reshape(groups, t_len, HALF_ROWS, 2, GROUP_SEQS, N_RWKV_HEADS).transpose(0, 4, 1, 5, 3, 2)
    return y.reshape(groups * GROUP_SEQS, t_len, D_RWKV)


def _state_to_lanes(s):
    bsz = s.shape[0]
    y = s.reshape(bsz // GROUP_SEQS, GROUP_SEQS, N_RWKV_HEADS, 2, HALF_ROWS, RWKV_HEAD).transpose(0, 5, 4, 3, 1, 2)
    return y.reshape(bsz // GROUP_SEQS, RWKV_HEAD, HALF_ROWS, LANES)


def _state_from_lanes(y):
    groups = y.shape[0]
    y = y.reshape(groups, RWKV_HEAD, HALF_ROWS, 2, GROUP_SEQS, N_RWKV_HEADS).transpose(0, 4, 5, 3, 2, 1)
    return y.reshape(groups * GROUP_SEQS, N_RWKV_HEADS, RWKV_HEAD, RWKV_HEAD)


def _scan(r, w, k, v, kk, b, s0, tc):
    bsz, t_len, _ = r.shape
    groups = bsz // GROUP_SEQS
    body = functools.partial(_scan_body, tc=tc)
    keys = pl.BlockSpec((1, tc, RWKV_HEAD, LANES), lambda i, c: (i, c, 0, 0))
    vals = pl.BlockSpec((1, tc, HALF_ROWS, LANES), lambda i, c: (i, c, 0, 0))
    st = pl.BlockSpec((1, RWKV_HEAD, HALF_ROWS, LANES), lambda i, c: (i, 0, 0, 0))
    o, s_fin = pl.pallas_call(
        body,
        grid=(groups, t_len // tc),
        in_specs=[keys, keys, keys, keys, keys, vals, st],
        out_specs=[vals, st],
        out_shape=[jax.ShapeDtypeStruct((groups, t_len, HALF_ROWS, LANES), F32),
                   jax.ShapeDtypeStruct((groups, RWKV_HEAD, HALF_ROWS, LANES), F32)],
        scratch_shapes=[pltpu.VMEM((RWKV_HEAD, HALF_ROWS, LANES), F32)],
        compiler_params=_cparams("parallel", "arbitrary"),
        name="scan",
    )(_keys_to_lanes(w), _keys_to_lanes(kk), _keys_to_lanes(b), _keys_to_lanes(k), _keys_to_lanes(r),
      _values_to_lanes(v), _state_to_lanes(s0))
    return _values_from_lanes(o), _state_from_lanes(s_fin)


def _postmix_body(x_ref, o_ref, bonus_ref, g_ref, yp_ref, lng_ref, lnb_ref, wout_ref, nfg_ref, bd_ref,
                  h_o, xnt_o):
    bd = bd_ref[...]
    o = o_ref[...]
    inv_n = 1.0 / RWKV_HEAD
    d = o - _seg_sum(o, bd) * inv_n
    var = _seg_sum(d * d, bd) * inv_n
    y_r = (d * lax.rsqrt(var + LNX_EPS) * lng_ref[...] + lnb_ref[...] + bonus_ref[...]) * g_ref[...]
    h = x_ref[...] + _bdot(y_r, wout_ref[:D_RWKV, :]) + _bdot(yp_ref[...], wout_ref[D_RWKV:, :])
    h_o[...] = h
    xnt_o[...] = _rms(h, nfg_ref[...]).T.astype(BF16)


def _postmix(x2d, o, bonus, g, yp, lw, w_out_bf16, bd, tile):
    rows = x2d.shape[0]
    full = lambda shape: pl.BlockSpec(shape, lambda i: (0,) * len(shape))
    half = pl.BlockSpec((tile, D_RWKV), lambda i: (i, 0))
    wide = pl.BlockSpec((tile, D_MODEL), lambda i: (i, 0))
    return pl.pallas_call(
        _postmix_body,
        grid=(rows // tile,),
        in_specs=[wide, half, half, half, half, full((1, D_RWKV)), full((1, D_RWKV)),
                  full((D_MODEL, D_MODEL)), full((1, D_MODEL)), full((D_RWKV, D_RWKV))],
        out_specs=[wide, pl.BlockSpec((D_MODEL, tile), lambda i: (0, i))],
        out_shape=[jax.ShapeDtypeStruct((rows, D_MODEL), F32), jax.ShapeDtypeStruct((D_MODEL, rows), BF16)],
        compiler_params=_cparams("parallel"),
        name="postmix",
    )(x2d, o, bonus, g, yp, lw["lnx_g"], lw["lnx_b"], w_out_bf16, lw["norm_ffn_g"], bd)


def _oddeven_mergesort_pairs(n):
    pairs = []
    p = 1
    while p < n:
        k = p
        while k >= 1:
            for j in range(k % p, n - k, 2 * k):
                for i in range(min(k, n - j - k)):
                    if (i + j) // (2 * p) == (i + j + k) // (2 * p):
                        pairs.append((i + j, i + j + k))
            k //= 2
        p *= 2
    return pairs


_SORT16 = _oddeven_mergesort_pairs(PEER_TOPK)


def _exchange(lst, i, j):
    lst[i], lst[j] = jnp.maximum(lst[i], lst[j]), jnp.minimum(lst[i], lst[j])


def _merge_top16(a, b):
    c = [jnp.maximum(a[i], b[PEER_TOPK - 1 - i]) for i in range(PEER_TOPK)]
    stride = PEER_TOPK // 2
    while stride:
        for i in range(PEER_TOPK):
            if not i & stride:
                _exchange(c, i, i + stride)
        stride //= 2
    return c


def _top16_of_keys(s):
    lst = [s[SUBLANES * g:SUBLANES * (g + 1), :] for g in range(N_KEYS // SUBLANES)]
    for i, j in _SORT16:
        _exchange(lst, i, j)
    shift = SUBLANES // 2
    while shift:
        lst = _merge_top16(lst, [pltpu.roll(x, shift, 0) for x in lst])
        shift //= 2
    return lst


def _route_body(xnt_ref, wqt_ref, keys_ref, s1_o, s2_o, e1_o, e2_o, tau_o, *, tile):
    qt = jnp.dot(wqt_ref[...], xnt_ref[...], preferred_element_type=F32).astype(BF16)
    head = lax.broadcasted_iota(jnp.int32, (PEER_HEADS, tile), 0)
    neg = jnp.full((PEER_HEADS, tile), NEG_INF, F32)
    top = [[neg] * PEER_TOPK, [neg] * PEER_TOPK]
    for h in range(PEER_HEADS):
        base = h * PEER_DK
        for half, s_o in enumerate((s1_o, s2_o)):
            q_half = qt[base + half * PEER_DK_HALF:base + (half + 1) * PEER_DK_HALF]
            s = jnp.dot(keys_ref[half], q_half, preferred_element_type=F32)
            s_o[h] = s
            best = _top16_of_keys(s)
            top[half] = [jnp.where(head == h, best[a], top[half][a]) for a in range(PEER_TOPK)]
    v1, v2 = top
    pad = lambda lst: lst + [neg] * (PEER_TOPK - len(lst))
    sc = [v1[0] + v2[b] for b in range(PEER_TOPK)]
    for a in range(1, PEER_TOPK // 2):
        sc = _merge_top16(sc, pad([v1[a] + v2[b] for b in range(PEER_TOPK // (a + 1))]))
    sc = _merge_top16(sc, pad([v1[a] + v2[0] for a in range(PEER_TOPK // 2, PEER_TOPK)]))
    z = jnp.ones_like(sc[0])
    for kth in sc[1:]:
        z = z + jnp.exp(kth - sc[0])
    tau_o[...] = sc[PEER_TOPK - 1]
    for h in range(PEER_HEADS):
        row = slice(h, h + 1)
        e1_o[h] = jnp.exp(s1_o[h] - v1[0][row])
        e2_o[h] = jnp.exp(s2_o[h] - v2[0][row]) / z[row]


def _route(xnt, wqt_bf16, keys_bf16, tile):
    rows = xnt.shape[1]
    body = functools.partial(_route_body, tile=tile)
    sco = pl.BlockSpec((PEER_HEADS, N_KEYS, tile), lambda i: (0, 0, i))
    sc_shape = jax.ShapeDtypeStruct((PEER_HEADS, N_KEYS, rows), F32)
    return pl.pallas_call(
        body,
        grid=(rows // tile,),
        in_specs=[
            pl.BlockSpec((D_MODEL, tile), lambda i: (0, i)),
            pl.BlockSpec((PEER_HEADS * PEER_DK, D_MODEL), lambda i: (0, 0)),
            pl.BlockSpec((2, N_KEYS, PEER_DK_HALF), lambda i: (0, 0, 0)),
        ],
        out_specs=[sco, sco, sco, sco, pl.BlockSpec((PEER_HEADS, tile), lambda i: (0, i))],
        out_shape=[sc_shape, sc_shape, sc_shape, sc_shape, jax.ShapeDtypeStruct((PEER_HEADS, rows), F32)],
        compiler_params=_cparams("parallel"),
        name="route",
    )(xnt, wqt_bf16, keys_bf16)


E1_PER_CHUNK = 8
CHUNK = E1_PER_CHUNK * N_KEYS


def _gelu(x):
    return 0.5 * x * (1.0 + lax.erf(x * (1.0 / math.sqrt(2.0))))


def _peer_body(xnt_ref, u_ref, vt_ref, s1_ref, s2_ref, e1_ref, e2_ref, tau_ref, out_o, acc_scr, a_scr, *, tile):
    c = pl.program_id(1)

    @pl.when(c == 0)
    def _():
        acc_scr[...] = jnp.zeros_like(acc_scr)

    ht = jnp.dot(u_ref[...], xnt_ref[...], preferred_element_type=F32)
    for e in range(E1_PER_CHUNK):
        e1 = c * E1_PER_CHUNK + e
        wgt = jnp.zeros((N_KEYS, tile), F32)
        for h in range(PEER_HEADS):
            s1row = s1_ref[h, pl.ds(e1, 1), :]
            e1row = e1_ref[h, pl.ds(e1, 1), :]
            sel = (s1row + s2_ref[h]) >= tau_ref[pl.ds(h, 1), :]
            wgt = wgt + jnp.where(sel, e1row * e2_ref[h], 0.0)
        rows = slice(e * N_KEYS, (e + 1) * N_KEYS)
        a_scr[rows, :] = (_gelu(ht[rows, :]) * wgt).astype(BF16)
    acc_scr[...] += jnp.dot(vt_ref[...], a_scr[...], preferred_element_type=F32)

    @pl.when(c == pl.num_programs(1) - 1)
    def _():
        out_o[...] = acc_scr[...].T


def _peer(xnt, u_bf16, vt_bf16, s1, s2, e1, e2, tau, tile):
    rows = xnt.shape[1]
    body = functools.partial(_peer_body, tile=tile)
    sco = pl.BlockSpec((PEER_HEADS, N_KEYS, tile), lambda i, c: (0, 0, i))
    return pl.pallas_call(
        body,
        grid=(rows // tile, N_EXPERTS // CHUNK),
        in_specs=[
            pl.BlockSpec((D_MODEL, tile), lambda i, c: (0, i)),
            pl.BlockSpec((CHUNK, D_MODEL), lambda i, c: (c, 0)),
            pl.BlockSpec((D_MODEL, CHUNK), lambda i, c: (0, c)),
            sco, sco, sco, sco,
            pl.BlockSpec((PEER_HEADS, tile), lambda i, c: (0, i)),
        ],
        out_specs=pl.BlockSpec((tile, D_MODEL), lambda i, c: (i, 0)),
        out_shape=jax.ShapeDtypeStruct((rows, D_MODEL), F32),
        scratch_shapes=[pltpu.VMEM((D_MODEL, tile), F32), pltpu.VMEM((CHUNK, tile), BF16)],
        compiler_params=_cparams("parallel", "arbitrary"),
        name="peer",
    )(xnt, u_bf16, vt_bf16, s1, s2, e1, e2, tau)


def _final_body(h_ref, peer_ref, p_ref, npg_ref, gatew_ref, plew_ref, fg_ref, y_o):
    h = h_ref[...] + peer_ref[...]
    gate = _sigmoid(_bdot(_rms(h, npg_ref[...]), gatew_ref[...]))
    h = h + _bdot(p_ref[...], plew_ref[...]) * gate
    y_o[...] = _rms(h, fg_ref[...])


def _final(h, peer, p2d, lw, gate_w_bf16, ple_w_bf16, final_g, tile):
    rows = h.shape[0]
    full = lambda shape: pl.BlockSpec(shape, lambda i: (0,) * len(shape))
    wide = pl.BlockSpec((tile, D_MODEL), lambda i: (i, 0))
    return pl.pallas_call(
        _final_body,
        grid=(rows // tile,),
        in_specs=[wide, wide, pl.BlockSpec((tile, PLE_DIM), lambda i: (i, 0)), full((1, D_MODEL)),
                  full((D_MODEL, D_MODEL)), full((PLE_DIM, D_MODEL)), full((1, D_MODEL))],
        out_specs=wide,
        out_shape=jax.ShapeDtypeStruct((rows, D_MODEL), F32),
        compiler_params=_cparams("parallel"),
        name="final",
    )(h, peer, p2d, lw["norm_ple_g"], gate_w_bf16, ple_w_bf16, final_g)


def _block_diag_ones():
    seg = jnp.arange(D_RWKV) // RWKV_HEAD
    return (seg[:, None] == seg[None, :]).astype(BF16)


def _tail(x2d, p2d, o, bonus, g, yp, lw, wts, bd, final_g, tile):
    h, xnt = _postmix(x2d, o, bonus, g, yp, lw, wts["w_out"], bd, tile)
    s1, s2, e1, e2, tau = _route(xnt, wts["wqt"], wts["keys"], tile)
    peer = _peer(xnt, wts["u"], wts["vt"], s1, s2, e1, e2, tau, tile)
    return _final(h, peer, p2d, lw, wts["gate_w"], wts["ple_w"], final_g, tile)


def kernel(x_prompt, x_sample, state_shift, state_wkv, state_pool, p_prompt, p_sample, norm_mix_g, w_in, shift_mu, decay_w0, decay_b, a_0, a_b, g_b, k_k, k_a, r_k, lnx_g, lnx_b, pool_w, pool_scale, w_out, norm_ffn_g, peer_wq, peer_keys, peer_u, peer_v, norm_ple_g, ple_w, ple_gate_w, final_norm_g):
    assert norm_mix_g.shape[0] == 1, "single trunk layer"
    bsz, seq, _ = x_prompt.shape
    dbsz, dseq, _ = x_sample.shape
    row = lambda a: a.reshape(1, -1)
    lw = dict(
        shift_mu=row(shift_mu[0]), decay_w0=row(decay_w0[0]), decay_b=decay_b[0], a_0=row(a_0[0]), a_b=a_b[0],
        g_b=g_b[0], k_k=row(k_k[0]), k_a=row(k_a[0]), r_k=row(r_k[0]), lnx_g=row(lnx_g[0]), lnx_b=row(lnx_b[0]),
        pool_w=pool_w[0], pool_scale=row(pool_scale[0]), norm_ffn_g=row(norm_ffn_g[0]),
        norm_ple_g=row(norm_ple_g[0]))
    wts = dict(
        w_out=w_out[0].astype(BF16), wqt=peer_wq[0].T.astype(BF16), keys=peer_keys[0].astype(BF16),
        u=peer_u[0].astype(BF16), vt=peer_v[0].T.astype(BF16), gate_w=ple_gate_w[0].astype(BF16),
        ple_w=ple_w[0].astype(BF16))
    w_in_bf16 = w_in[0].astype(BF16)
    mix_g = row(norm_mix_g[0])
    final_g = row(final_norm_g)
    bd = _block_diag_ones()
    tile = 512

    xp = x_prompt.reshape(bsz * seq, D_MODEL)
    zp = _proj(xp, mix_g, w_in_bf16, tile)
    pm_tile = 256
    r, w, k, v, kk, b, g, bonus, yp = _premix(zp, lw, bd, pm_tile, seq // pm_tile)
    seq3 = lambda a: a.reshape(bsz, seq, D_RWKV)
    s0 = jnp.zeros((bsz, N_RWKV_HEADS, RWKV_HEAD, RWKV_HEAD), F32)
    o, s_fin = _scan(seq3(r), seq3(w), seq3(k), seq3(v), seq3(kk), seq3(b), s0, 64)
    y_prompt = _tail(xp, p_prompt[0].reshape(bsz * seq, PLE_DIM), o.reshape(bsz * seq, D_RWKV), bonus, g, yp,
                     lw, wts, bd, final_g, tile).reshape(bsz, seq, D_MODEL)
    zp3 = zp.reshape(bsz, seq, D_IN)
    shift_prompt = zp3[:, -1, :D_SHIFT][None]
    wkv_prompt = s_fin[None]
    pool_prompt = zp3[:, seq - POOL_BUF:, D_SHIFT:][None]

    xs = x_sample.reshape(dbsz * dseq, D_MODEL)
    zs = _proj(xs, mix_g, w_in_bf16, dbsz * dseq).reshape(dbsz, dseq, D_IN)
    hist = jnp.zeros((dbsz, HALO, D_IN), F32)
    hist = hist.at[:, HALO - POOL_BUF:, D_SHIFT:].set(state_pool[0])
    hist = hist.at[:, HALO - 1, :D_SHIFT].set(state_shift[0])
    ext_len = HALO + dseq
    z_ext = jnp.concatenate([hist, zs], axis=1).reshape(dbsz * ext_len, D_IN)
    outs = _premix(z_ext, lw, bd, 512, 0)
    keep = lambda a: a.reshape(dbsz, ext_len, D_RWKV)[:, HALO:, :]
    r, w, k, v, kk, b, g, bonus, yp = [keep(a) for a in outs]
    o, s_fin = _scan(r, w, k, v, kk, b, state_wkv[0], dseq)
    flat = lambda a: a.reshape(dbsz * dseq, D_RWKV)
    y_sample = _tail(xs, p_sample[0].reshape(dbsz * dseq, PLE_DIM), flat(o), flat(bonus), flat(g), flat(yp),
                     lw, wts, bd, final_g, tile).reshape(dbsz, dseq, D_MODEL)
    shift_sample = zs[:, -1, :D_SHIFT][None]
    wkv_sample = s_fin[None]
    pool_sample = jnp.concatenate([state_pool[0], zs[:, :, D_SHIFT:]], axis=1)[:, -POOL_BUF:][None]

    return (y_prompt, y_sample, shift_prompt, wkv_prompt, pool_prompt, shift_sample, wkv_sample, pool_sample)
```

```python
import functools
import math

import jax
import jax.numpy as jnp
from jax import lax
from jax.experimental import pallas as pl
from jax.experimental.pallas import tpu as pltpu

F32 = jnp.float32
BF16 = jnp.bfloat16

D_MODEL = 1024
RWKV_HEAD = 64
D_RWKV = 512
N_RWKV_HEADS = 8
DECAY_LORA = 64
AAA_LORA = 64
GATE_LORA = 128
D_SHIFT = 3 * D_RWKV + DECAY_LORA + AAA_LORA + GATE_LORA
D_POOL = 512
POOL_WINDOWS = (2, 4, 8, 16)
POOL_GROUP = 128
POOL_BUF = 15
D_IN = D_SHIFT + D_POOL
N_KEYS = 128
N_EXPERTS = N_KEYS * N_KEYS
PEER_HEADS = 8
PEER_TOPK = 16
PEER_DK = 256
PEER_DK_HALF = 128
PLE_DIM = 256
NORM_EPS = 1e-6
LNX_EPS = 64e-5
PAST_LEN = 16384

HALO = 16
LANES = 128
SUBLANES = 8
VMEM_LIMIT = 48 * 1024 * 1024

NEG_INF = float("-inf")


def _cparams(*sem):
    return pltpu.CompilerParams(dimension_semantics=sem, vmem_limit_bytes=VMEM_LIMIT)


def _rms(x, g):
    return x * lax.rsqrt(jnp.mean(x * x, axis=-1, keepdims=True) + NORM_EPS) * g


def _sigmoid(x):
    return 1.0 / (1.0 + jnp.exp(-x))


def _bdot(a, b):
    return jnp.dot(a.astype(BF16), b.astype(BF16), preferred_element_type=F32)


def _seg_sum(x, bd):
    x1 = x.astype(BF16)
    r1 = x - x1.astype(F32)
    x2 = r1.astype(BF16)
    x3 = (r1 - x2.astype(F32)).astype(BF16)
    dot = functools.partial(jnp.dot, preferred_element_type=F32)
    return dot(x1, bd) + dot(x2, bd) + dot(x3, bd)


def _proj_body(x_ref, g_ref, w_ref, z_ref):
    z_ref[...] = _bdot(_rms(x_ref[...], g_ref[...]), w_ref[...])


def _proj(x2d, g, w_bf16, tile):
    rows = x2d.shape[0]
    return pl.pallas_call(
        _proj_body,
        grid=(rows // tile,),
        in_specs=[
            pl.BlockSpec((tile, D_MODEL), lambda i: (i, 0)),
            pl.BlockSpec((1, D_MODEL), lambda i: (0, 0)),
            pl.BlockSpec((D_MODEL, D_IN), lambda i: (0, 0)),
        ],
        out_specs=pl.BlockSpec((tile, D_IN), lambda i: (i, 0)),
        out_shape=jax.ShapeDtypeStruct((rows, D_IN), F32),
        compiler_params=_cparams("parallel"),
        name="proj",
    )(x2d, g, w_bf16)


def _premix_body(z_ref, halo_ref, mu_ref, w0_ref, decb_ref, a0_ref, ab_ref, gb_ref, kk_ref, ka_ref,
                 rk_ref, poolw_ref, pools_ref, bd_ref,
                 r_o, w_o, k_o, v_o, kkn_o, b_o, g_o, bonus_o, yp_o, ext_scr, *, tile, seq_tiles):
    halo = halo_ref[...]
    if seq_tiles:
        ti = pl.program_id(0) % seq_tiles
        halo = jnp.where(ti == 0, 0.0, halo)
    ext_scr[0:HALO, :] = halo
    ext_scr[HALO:HALO + tile, :] = z_ref[...]

    zr = z_ref[:, :D_SHIFT]
    zp = ext_scr[HALO - 1:HALO - 1 + tile, :D_SHIFT]
    zs = zr + (zp - zr) * mu_ref[...]
    r = zs[:, :D_RWKV]
    k = zs[:, D_RWKV:2 * D_RWKV]
    v = zs[:, 2 * D_RWKV:3 * D_RWKV]
    o = 3 * D_RWKV
    zw = zs[:, o:o + DECAY_LORA]
    za = zs[:, o + DECAY_LORA:o + DECAY_LORA + AAA_LORA]
    zg = zs[:, o + DECAY_LORA + AAA_LORA:]
    dec_in = w0_ref[...] + _bdot(jnp.tanh(zw), decb_ref[...])
    nx = -dec_in
    softplus = jnp.maximum(nx, 0.0) + jnp.log1p(jnp.exp(-jnp.abs(nx)))
    decay = jnp.exp(-jnp.exp(-softplus - 0.5))
    a = _sigmoid(a0_ref[...] + _bdot(za, ab_ref[...]))
    g = _bdot(_sigmoid(zg), gb_ref[...])
    bd = bd_ref[...]
    kkf = k * kk_ref[...]
    kkn = kkf / jnp.maximum(jnp.sqrt(_seg_sum(kkf * kkf, bd)), 1e-12)
    k2 = k * (1.0 + (a - 1.0) * ka_ref[...])
    bonus = _seg_sum(r * k2 * rk_ref[...], bd) * v
    r_o[...] = r.T
    w_o[...] = decay.T
    k_o[...] = k2.T
    v_o[...] = v
    kkn_o[...] = kkn.T
    b_o[...] = (kkn * a).T
    g_o[...] = g
    bonus_o[...] = bonus

    if seq_tiles:
        pos = ti * tile + lax.broadcasted_iota(jnp.int32, (tile, 1), 0)
    for gi, wdw in enumerate(POOL_WINDOWS):
        cols = slice(D_SHIFT + gi * POOL_GROUP, D_SHIFT + (gi + 1) * POOL_GROUP)
        u_ext = ext_scr[:, cols]
        s = u_ext
        shift = 1
        while shift < wdw:
            s = s + pltpu.roll(s, shift, 0)
            shift *= 2
        u = u_ext[HALO:, :]
        if seq_tiles:
            cnt = jnp.minimum(pos + 1, wdw).astype(F32)
        else:
            cnt = float(wdw)
        pooled = s[HALO:, :] / cnt - u
        y = _bdot(pooled, poolw_ref[gi])
        yp_o[:, gi * POOL_GROUP:(gi + 1) * POOL_GROUP] = y * pools_ref[:, gi * POOL_GROUP:(gi + 1) * POOL_GROUP]


def _premix(z, lw, bd, tile, seq_tiles):
    rows = z.shape[0]
    body = functools.partial(_premix_body, tile=tile, seq_tiles=seq_tiles)
    full = lambda shape: pl.BlockSpec(shape, lambda i: (0,) * len(shape))
    out_block = pl.BlockSpec((tile, D_RWKV), lambda i: (i, 0))
    out = jax.ShapeDtypeStruct((rows, D_RWKV), F32)
    out_block_t = pl.BlockSpec((D_RWKV, tile), lambda i: (0, i))
    out_t = jax.ShapeDtypeStruct((D_RWKV, rows), F32)
    transposed = (True, True, True, False, True, True, False, False, False)
    halo_blocks = tile // HALO
    return pl.pallas_call(
        body,
        grid=(rows // tile,),
        in_specs=[
            pl.BlockSpec((tile, D_IN), lambda i: (i, 0)),
            pl.BlockSpec((HALO, D_IN), lambda i: (jnp.maximum(i * halo_blocks - 1, 0), 0)),
            full((1, D_SHIFT)), full((1, D_RWKV)), full((DECAY_LORA, D_RWKV)), full((1, D_RWKV)),
            full((AAA_LORA, D_RWKV)), full((GATE_LORA, D_RWKV)), full((1, D_RWKV)), full((1, D_RWKV)),
            full((1, D_RWKV)), full((4, POOL_GROUP, POOL_GROUP)), full((1, D_POOL)), full((D_RWKV, D_RWKV)),
        ],
        out_specs=[out_block_t if t else out_block for t in transposed],
        out_shape=[out_t if t else out for t in transposed],
        scratch_shapes=[pltpu.VMEM((tile + HALO, D_IN), F32)],
        compiler_params=_cparams("parallel"),
        name="premix",
    )(z, z, lw["shift_mu"], lw["decay_w0"], lw["decay_b"], lw["a_0"], lw["a_b"], lw["g_b"], lw["k_k"],
      lw["k_a"], lw["r_k"], lw["pool_w"], lw["pool_scale"], bd)


GROUP_SEQS = LANES // (2 * N_RWKV_HEADS)
HALF_ROWS = RWKV_HEAD // 2
N_ACC = 4


def _tree_sum(parts):
    while len(parts) > 1:
        parts = [a + b for a, b in zip(parts[::2], parts[1::2])]
    return parts[0]


def _scan_body(w_ref, kk_ref, b_ref, k_ref, r_ref, v_ref, s0_ref, o_ref, sout_ref, st_scr, *, tc):
    c = pl.program_id(1)

    @pl.when(c == 0)
    def _():
        st_scr[...] = s0_ref[0]

    def step(t, carry):
        acc = [None] * N_ACC
        for j in range(RWKV_HEAD):
            term = st_scr[j] * kk_ref[0, t, pl.ds(j, 1), :]
            acc[j % N_ACC] = term if acc[j % N_ACC] is None else acc[j % N_ACC] + term
        sk = _tree_sum(acc)
        v = v_ref[0, t]
        acc = [None] * N_ACC
        for j in range(RWKV_HEAD):
            row_j = pl.ds(j, 1)
            s = (st_scr[j] * w_ref[0, t, row_j, :] - sk * b_ref[0, t, row_j, :]) + v * k_ref[0, t, row_j, :]
            st_scr[j] = s
            term = s * r_ref[0, t, row_j, :]
            acc[j % N_ACC] = term if acc[j % N_ACC] is None else acc[j % N_ACC] + term
        o_ref[0, t] = _tree_sum(acc)
        return carry

    if tc < SUBLANES:
        for t in range(tc):
            step(t, 0)
    else:
        lax.fori_loop(0, tc, step, 0)

    @pl.when(c == pl.num_programs(1) - 1)
    def _():
        sout_ref[0] = st_scr[...]


def _keys_to_lanes(xt):
    _, bsz, t_len = xt.shape
    y = xt.reshape(1, N_RWKV_HEADS, RWKV_HEAD, bsz // GROUP_SEQS, GROUP_SEQS, t_len)
    y = jnp.broadcast_to(y, (2, N_RWKV_HEADS, RWKV_HEAD, bsz // GROUP_SEQS, GROUP_SEQS, t_len))
    return y.transpose(3, 5, 2, 0, 4, 1).reshape(bsz // GROUP_SEQS, t_len, RWKV_HEAD, LANES)


def _values_to_lanes(x):
    bsz, t_len, _ = x.shape
    y = x.reshape(bsz // GROUP_SEQS, GROUP_SEQS, t_len, N_RWKV_HEADS, 2, HALF_ROWS).transpose(0, 2, 5, 4, 1, 3)
    return y.reshape(bsz // GROUP_SEQS, t_len, HALF_ROWS, LANES)


def _values_from_lanes(y):
    groups, t_len = y.shape[:2]
    y = y.reshape(groups, t_len, HALF_ROWS, 2, GROUP_SEQS, N_RWKV_HEADS).transpose(0, 4, 1, 5, 3, 2)
    return y.reshape(groups * GROUP_SEQS, t_len, D_RWKV)


def _state_to_lanes(s):
    bsz = s.shape[0]
    y = s.reshape(bsz // GROUP_SEQS, GROUP_SEQS, N_RWKV_HEADS, 2, HALF_ROWS, RWKV_HEAD).transpose(0, 5, 4, 3, 1, 2)
    return y.reshape(bsz // GROUP_SEQS, RWKV_HEAD, HALF_ROWS, LANES)


def _state_from_lanes(y):
    groups = y.shape[0]
    y = y.reshape(groups, RWKV_HEAD, HALF_ROWS, 2, GROUP_SEQS, N_RWKV_HEADS).transpose(0, 4, 5, 3, 2, 1)
    return y.reshape(groups * GROUP_SEQS, N_RWKV_HEADS, RWKV_HEAD, RWKV_HEAD)


def _scan(r, w, k, v, kk, b, s0, tc):
    bsz, t_len, _ = v.shape
    groups = bsz // GROUP_SEQS
    body = functools.partial(_scan_body, tc=tc)
    keys = pl.BlockSpec((1, tc, RWKV_HEAD, LANES), lambda i, c: (i, c, 0, 0))
    vals = pl.BlockSpec((1, tc, HALF_ROWS, LANES), lambda i, c: (i, c, 0, 0))
    st = pl.BlockSpec((1, RWKV_HEAD, HALF_ROWS, LANES), lambda i, c: (i, 0, 0, 0))
    o, s_fin = pl.pallas_call(
        body,
        grid=(groups, t_len // tc),
        in_specs=[keys, keys, keys, keys, keys, vals, st],
        out_specs=[vals, st],
        out_shape=[jax.ShapeDtypeStruct((groups, t_len, HALF_ROWS, LANES), F32),
                   jax.ShapeDtypeStruct((groups, RWKV_HEAD, HALF_ROWS, LANES), F32)],
        scratch_shapes=[pltpu.VMEM((RWKV_HEAD, HALF_ROWS, LANES), F32)],
        compiler_params=_cparams("parallel", "arbitrary"),
        name="scan",
    )(_keys_to_lanes(w), _keys_to_lanes(kk), _keys_to_lanes(b), _keys_to_lanes(k), _keys_to_lanes(r),
      _values_to_lanes(v), _state_to_lanes(s0))
    return _values_from_lanes(o), _state_from_lanes(s_fin)


def _postmix_body(x_ref, o_ref, bonus_ref, g_ref, yp_ref, lng_ref, lnb_ref, wout_ref, nfg_ref, bd_ref,
                  h_o, xnt_o):
    bd = bd_ref[...]
    o = o_ref[...]
    inv_n = 1.0 / RWKV_HEAD
    d = o - _seg_sum(o, bd) * inv_n
    var = _seg_sum(d * d, bd) * inv_n
    y_r = (d * lax.rsqrt(var + LNX_EPS) * lng_ref[...] + lnb_ref[...] + bonus_ref[...]) * g_ref[...]
    h = x_ref[...] + _bdot(y_r, wout_ref[:D_RWKV, :]) + _bdot(yp_ref[...], wout_ref[D_RWKV:, :])
    h_o[...] = h
    xnt_o[...] = _rms(h, nfg_ref[...]).T.astype(BF16)


def _postmix(x2d, o, bonus, g, yp, lw, w_out_bf16, bd, tile):
    rows = x2d.shape[0]
    full = lambda shape: pl.BlockSpec(shape, lambda i: (0,) * len(shape))
    half = pl.BlockSpec((tile, D_RWKV), lambda i: (i, 0))
    wide = pl.BlockSpec((tile, D_MODEL), lambda i: (i, 0))
    return pl.pallas_call(
        _postmix_body,
        grid=(rows // tile,),
        in_specs=[wide, half, half, half, half, full((1, D_RWKV)), full((1, D_RWKV)),
                  full((D_MODEL, D_MODEL)), full((1, D_MODEL)), full((D_RWKV, D_RWKV))],
        out_specs=[wide, pl.BlockSpec((D_MODEL, tile), lambda i: (0, i))],
        out_shape=[jax.ShapeDtypeStruct((rows, D_MODEL), F32), jax.ShapeDtypeStruct((D_MODEL, rows), BF16)],
        compiler_params=_cparams("parallel"),
        name="postmix",
    )(x2d, o, bonus, g, yp, lw["lnx_g"], lw["lnx_b"], w_out_bf16, lw["norm_ffn_g"], bd)


def _oddeven_mergesort_pairs(n):
    pairs = []
    p = 1
    while p < n:
        k = p
        while k >= 1:
            for j in range(k % p, n - k, 2 * k):
                for i in range(min(k, n - j - k)):
                    if (i + j) // (2 * p) == (i + j + k) // (2 * p):
                        pairs.append((i + j, i + j + k))
            k //= 2
        p *= 2
    return pairs


_SORT16 = _oddeven_mergesort_pairs(PEER_TOPK)


def _exchange(lst, i, j):
    lst[i], lst[j] = jnp.maximum(lst[i], lst[j]), jnp.minimum(lst[i], lst[j])


def _merge_top16(a, b):
    c = [jnp.maximum(a[i], b[PEER_TOPK - 1 - i]) for i in range(PEER_TOPK)]
    stride = PEER_TOPK // 2
    while stride:
        for i in range(PEER_TOPK):
            if not i & stride:
                _exchange(c, i, i + stride)
        stride //= 2
    return c


def _top16_of_keys(s):
    lst = [s[SUBLANES * g:SUBLANES * (g + 1), :] for g in range(N_KEYS // SUBLANES)]
    for i, j in _SORT16:
        _exchange(lst, i, j)
    shift = SUBLANES // 2
    while shift:
        lst = _merge_top16(lst, [pltpu.roll(x, shift, 0) for x in lst])
        shift //= 2
    return lst


def _route_body(xnt_ref, wqt_ref, keys_ref, s1_o, s2_o, e1_o, e2_o, tau_o, *, tile):
    qt = jnp.dot(wqt_ref[...], xnt_ref[...], preferred_element_type=F32).astype(BF16)
    head = lax.broadcasted_iota(jnp.int32, (PEER_HEADS, tile), 0)
    neg = jnp.full((PEER_HEADS, tile), NEG_INF, F32)
    top = [[neg] * PEER_TOPK, [neg] * PEER_TOPK]
    for h in range(PEER_HEADS):
        base = h * PEER_DK
        for half, s_o in enumerate((s1_o, s2_o)):
            q_half = qt[base + half * PEER_DK_HALF:base + (half + 1) * PEER_DK_HALF]
            s = jnp.dot(keys_ref[half], q_half, preferred_element_type=F32)
            s_o[h] = s
            best = _top16_of_keys(s)
            top[half] = [jnp.where(head == h, best[a], top[half][a]) for a in range(PEER_TOPK)]
    v1, v2 = top
    pad = lambda lst: lst + [neg] * (PEER_TOPK - len(lst))
    sc = [v1[0] + v2[b] for b in range(PEER_TOPK)]
    for a in range(1, PEER_TOPK // 2):
        sc = _merge_top16(sc, pad([v1[a] + v2[b] for b in range(PEER_TOPK // (a + 1))]))
    sc = _merge_top16(sc, pad([v1[a] + v2[0] for a in range(PEER_TOPK // 2, PEER_TOPK)]))
    z = jnp.ones_like(sc[0])
    for kth in sc[1:]:
        z = z + jnp.exp(kth - sc[0])
    tau_o[...] = sc[PEER_TOPK - 1]
    for h in range(PEER_HEADS):
        row = slice(h, h + 1)
        e1_o[h] = jnp.exp(s1_o[h] - v1[0][row])
        e2_o[h] = jnp.exp(s2_o[h] - v2[0][row]) / z[row]


def _route(xnt, wqt_bf16, keys_bf16, tile):
    rows = xnt.shape[1]
    body = functools.partial(_route_body, tile=tile)
    sco = pl.BlockSpec((PEER_HEADS, N_KEYS, tile), lambda i: (0, 0, i))
    sc_shape = jax.ShapeDtypeStruct((PEER_HEADS, N_KEYS, rows), F32)
    return pl.pallas_call(
        body,
        grid=(rows // tile,),
        in_specs=[
            pl.BlockSpec((D_MODEL, tile), lambda i: (0, i)),
            pl.BlockSpec((PEER_HEADS * PEER_DK, D_MODEL), lambda i: (0, 0)),
            pl.BlockSpec((2, N_KEYS, PEER_DK_HALF), lambda i: (0, 0, 0)),
        ],
        out_specs=[sco, sco, sco, sco, pl.BlockSpec((PEER_HEADS, tile), lambda i: (0, i))],
        out_shape=[sc_shape, sc_shape, sc_shape, sc_shape, jax.ShapeDtypeStruct((PEER_HEADS, rows), F32)],
        compiler_params=_cparams("parallel"),
        name="route",
    )(xnt, wqt_bf16, keys_bf16)


E1_PER_CHUNK = 8
CHUNK = E1_PER_CHUNK * N_KEYS


def _gelu(x):
    return 0.5 * x * (1.0 + lax.erf(x * (1.0 / math.sqrt(2.0))))


def _peer_body(xnt_ref, u_ref, vt_ref, s1_ref, s2_ref, e1_ref, e2_ref, tau_ref, out_o, acc_scr, a_scr, *, tile):
    c = pl.program_id(1)

    @pl.when(c == 0)
    def _():
        acc_scr[...] = jnp.zeros_like(acc_scr)

    ht = jnp.dot(u_ref[...], xnt_ref[...], preferred_element_type=F32)
    for e in range(E1_PER_CHUNK):
        e1 = c * E1_PER_CHUNK + e
        wgt = None
        for h in range(PEER_HEADS):
            s1row = s1_ref[h, pl.ds(e1, 1), :]
            e1row = e1_ref[h, pl.ds(e1, 1), :]
            sel = (s1row + s2_ref[h]) >= tau_ref[pl.ds(h, 1), :]
            term = jnp.where(sel, e1row * e2_ref[h], 0.0)
            wgt = term if wgt is None else wgt + term
        rows = slice(e * N_KEYS, (e + 1) * N_KEYS)
        a_scr[rows, :] = (_gelu(ht[rows, :]) * wgt).astype(BF16)
    acc_scr[...] += jnp.dot(vt_ref[...], a_scr[...], preferred_element_type=F32)

    @pl.when(c == pl.num_programs(1) - 1)
    def _():
        out_o[...] = acc_scr[...].T


def _peer(xnt, u_bf16, vt_bf16, s1, s2, e1, e2, tau, tile):
    rows = xnt.shape[1]
    body = functools.partial(_peer_body, tile=tile)
    sco = pl.BlockSpec((PEER_HEADS, N_KEYS, tile), lambda i, c: (0, 0, i))
    return pl.pallas_call(
        body,
        grid=(rows // tile, N_EXPERTS // CHUNK),
        in_specs=[
            pl.BlockSpec((D_MODEL, tile), lambda i, c: (0, i)),
            pl.BlockSpec((CHUNK, D_MODEL), lambda i, c: (c, 0)),
            pl.BlockSpec((D_MODEL, CHUNK), lambda i, c: (0, c)),
            sco, sco, sco, sco,
            pl.BlockSpec((PEER_HEADS, tile), lambda i, c: (0, i)),
        ],
        out_specs=pl.BlockSpec((tile, D_MODEL), lambda i, c: (i, 0)),
        out_shape=jax.ShapeDtypeStruct((rows, D_MODEL), F32),
        scratch_shapes=[pltpu.VMEM((D_MODEL, tile), F32), pltpu.VMEM((CHUNK, tile), BF16)],
        compiler_params=_cparams("parallel", "arbitrary"),
        name="peer",
    )(xnt, u_bf16, vt_bf16, s1, s2, e1, e2, tau)


def _final_body(h_ref, peer_ref, p_ref, npg_ref, gatew_ref, plew_ref, fg_ref, y_o):
    h = h_ref[...] + peer_ref[...]
    gate = _sigmoid(_bdot(_rms(h, npg_ref[...]), gatew_ref[...]))
    h = h + _bdot(p_ref[...], plew_ref[...]) * gate
    y_o[...] = _rms(h, fg_ref[...])


def _final(h, peer, p2d, lw, gate_w_bf16, ple_w_bf16, final_g, tile):
    rows = h.shape[0]
    full = lambda shape: pl.BlockSpec(shape, lambda i: (0,) * len(shape))
    wide = pl.BlockSpec((tile, D_MODEL), lambda i: (i, 0))
    return pl.pallas_call(
        _final_body,
        grid=(rows // tile,),
        in_specs=[wide, wide, pl.BlockSpec((tile, PLE_DIM), lambda i: (i, 0)), full((1, D_MODEL)),
                  full((D_MODEL, D_MODEL)), full((PLE_DIM, D_MODEL)), full((1, D_MODEL))],
        out_specs=wide,
        out_shape=jax.ShapeDtypeStruct((rows, D_MODEL), F32),
        compiler_params=_cparams("parallel"),
        name="final",
    )(h, peer, p2d, lw["norm_ple_g"], gate_w_bf16, ple_w_bf16, final_g)


def _block_diag_ones():
    seg = jnp.arange(D_RWKV) // RWKV_HEAD
    return (seg[:, None] == seg[None, :]).astype(BF16)


def _tail(x2d, p2d, o, bonus, g, yp, lw, wts, bd, final_g, tile):
    h, xnt = _postmix(x2d, o, bonus, g, yp, lw, wts["w_out"], bd, tile)
    s1, s2, e1, e2, tau = _route(xnt, wts["wqt"], wts["keys"], tile)
    peer = _peer(xnt, wts["u"], wts["vt"], s1, s2, e1, e2, tau, tile)
    return _final(h, peer, p2d, lw, wts["gate_w"], wts["ple_w"], final_g, tile)


def kernel(x_prompt, x_sample, state_shift, state_wkv, state_pool, p_prompt, p_sample, norm_mix_g, w_in, shift_mu, decay_w0, decay_b, a_0, a_b, g_b, k_k, k_a, r_k, lnx_g, lnx_b, pool_w, pool_scale, w_out, norm_ffn_g, peer_wq, peer_keys, peer_u, peer_v, norm_ple_g, ple_w, ple_gate_w, final_norm_g):
    assert norm_mix_g.shape[0] == 1, "single trunk layer"
    bsz, seq, _ = x_prompt.shape
    dbsz, dseq, _ = x_sample.shape
    row = lambda a: a.reshape(1, -1)
    lw = dict(
        shift_mu=row(shift_mu[0]), decay_w0=row(decay_w0[0]), decay_b=decay_b[0], a_0=row(a_0[0]), a_b=a_b[0],
        g_b=g_b[0], k_k=row(k_k[0]), k_a=row(k_a[0]), r_k=row(r_k[0]), lnx_g=row(lnx_g[0]), lnx_b=row(lnx_b[0]),
        pool_w=pool_w[0], pool_scale=row(pool_scale[0]), norm_ffn_g=row(norm_ffn_g[0]),
        norm_ple_g=row(norm_ple_g[0]))
    wts = dict(
        w_out=w_out[0].astype(BF16), wqt=peer_wq[0].T.astype(BF16), keys=peer_keys[0].astype(BF16),
        u=peer_u[0].astype(BF16), vt=peer_v[0].T.astype(BF16), gate_w=ple_gate_w[0].astype(BF16),
        ple_w=ple_w[0].astype(BF16))
    w_in_bf16 = w_in[0].astype(BF16)
    mix_g = row(norm_mix_g[0])
    final_g = row(final_norm_g)
    bd = _block_diag_ones()
    tile = 512

    xp = x_prompt.reshape(bsz * seq, D_MODEL)
    zp = _proj(xp, mix_g, w_in_bf16, tile)
    pm_tile = 256
    r, w, k, v, kk, b, g, bonus, yp = _premix(zp, lw, bd, pm_tile, seq // pm_tile)
    seq3 = lambda a: a.reshape(D_RWKV, bsz, seq)
    s0 = jnp.zeros((bsz, N_RWKV_HEADS, RWKV_HEAD, RWKV_HEAD), F32)
    o, s_fin = _scan(seq3(r), seq3(w), seq3(k), v.reshape(bsz, seq, D_RWKV), seq3(kk), seq3(b), s0, 64)
    y_prompt = _tail(xp, p_prompt[0].reshape(bsz * seq, PLE_DIM), o.reshape(bsz * seq, D_RWKV), bonus, g, yp,
                     lw, wts, bd, final_g, tile).reshape(bsz, seq, D_MODEL)
    zp3 = zp.reshape(bsz, seq, D_IN)
    shift_prompt = zp3[:, -1, :D_SHIFT][None]
    wkv_prompt = s_fin[None]
    pool_prompt = zp3[:, seq - POOL_BUF:, D_SHIFT:][None]

    xs = x_sample.reshape(dbsz * dseq, D_MODEL)
    zs = _proj(xs, mix_g, w_in_bf16, dbsz * dseq).reshape(dbsz, dseq, D_IN)
    hist = jnp.zeros((dbsz, HALO, D_IN), F32)
    hist = hist.at[:, HALO - POOL_BUF:, D_SHIFT:].set(state_pool[0])
    hist = hist.at[:, HALO - 1, :D_SHIFT].set(state_shift[0])
    ext_len = HALO + dseq
    z_ext = jnp.concatenate([hist, zs], axis=1).reshape(dbsz * ext_len, D_IN)
    outs = _premix(z_ext, lw, bd, 512, 0)
    keep = lambda a: a.reshape(dbsz, ext_len, D_RWKV)[:, HALO:, :]
    keep_t = lambda a: a.reshape(D_RWKV, dbsz, ext_len)[:, :, HALO:]
    r, w, k, kk, b = [keep_t(outs[i]) for i in (0, 1, 2, 4, 5)]
    v, g, bonus, yp = [keep(outs[i]) for i in (3, 6, 7, 8)]
    o, s_fin = _scan(r, w, k, v, kk, b, state_wkv[0], dseq)
    flat = lambda a: a.reshape(dbsz * dseq, D_RWKV)
    y_sample = _tail(xs, p_sample[0].reshape(dbsz * dseq, PLE_DIM), flat(o), flat(bonus), flat(g), flat(yp),
                     lw, wts, bd, final_g, tile).reshape(dbsz, dseq, D_MODEL)
    shift_sample = zs[:, -1, :D_SHIFT][None]
    wkv_sample = s_fin[None]
    pool_sample = jnp.concatenate([state_pool[0], zs[:, :, D_SHIFT:]], axis=1)[:, -POOL_BUF:][None]

    return (y_prompt, y_sample, shift_prompt, wkv_prompt, pool_prompt, shift_sample, wkv_sample, pool_sample)
```

```python
import functools
import math

import jax
import jax.numpy as jnp
from jax import lax
from jax.experimental import pallas as pl
from jax.experimental.pallas import tpu as pltpu

F32 = jnp.float32
BF16 = jnp.bfloat16

D_MODEL = 1024
RWKV_HEAD = 64
D_RWKV = 512
N_RWKV_HEADS = 8
DECAY_LORA = 64
AAA_LORA = 64
GATE_LORA = 128
D_SHIFT = 3 * D_RWKV + DECAY_LORA + AAA_LORA + GATE_LORA
D_POOL = 512
POOL_WINDOWS = (2, 4, 8, 16)
POOL_GROUP = 128
POOL_BUF = 15
D_IN = D_SHIFT + D_POOL
N_KEYS = 128
N_EXPERTS = N_KEYS * N_KEYS
PEER_HEADS = 8
PEER_TOPK = 16
PEER_DK = 256
PEER_DK_HALF = 128
PLE_DIM = 256
NORM_EPS = 1e-6
LNX_EPS = 64e-5
PAST_LEN = 16384

HALO = 16
LANES = 128
SUBLANES = 8
VMEM_LIMIT = 48 * 1024 * 1024

NEG_INF = float("-inf")


def _cparams(*sem):
    return pltpu.CompilerParams(dimension_semantics=sem, vmem_limit_bytes=VMEM_LIMIT)


def _rms(x, g):
    return x * lax.rsqrt(jnp.mean(x * x, axis=-1, keepdims=True) + NORM_EPS) * g


def _sigmoid(x):
    return 1.0 / (1.0 + jnp.exp(-x))


def _bdot(a, b):
    return jnp.dot(a.astype(BF16), b.astype(BF16), preferred_element_type=F32)


def _seg_sum(x, bd):
    x1 = x.astype(BF16)
    r1 = x - x1.astype(F32)
    x2 = r1.astype(BF16)
    x3 = (r1 - x2.astype(F32)).astype(BF16)
    dot = functools.partial(jnp.dot, preferred_element_type=F32)
    return dot(x1, bd) + dot(x2, bd) + dot(x3, bd)


def _proj_body(x_ref, g_ref, w_ref, z_ref):
    z_ref[...] = _bdot(_rms(x_ref[...], g_ref[...]), w_ref[...])


def _proj(x2d, g, w_bf16, tile):
    rows = x2d.shape[0]
    return pl.pallas_call(
        _proj_body,
        grid=(rows // tile,),
        in_specs=[
            pl.BlockSpec((tile, D_MODEL), lambda i: (i, 0)),
            pl.BlockSpec((1, D_MODEL), lambda i: (0, 0)),
            pl.BlockSpec((D_MODEL, D_IN), lambda i: (0, 0)),
        ],
        out_specs=pl.BlockSpec((tile, D_IN), lambda i: (i, 0)),
        out_shape=jax.ShapeDtypeStruct((rows, D_IN), F32),
        compiler_params=_cparams("parallel"),
        name="proj",
    )(x2d, g, w_bf16)


def _premix_body(z_ref, halo_ref, mu_ref, w0_ref, decb_ref, a0_ref, ab_ref, gb_ref, kk_ref, ka_ref,
                 rk_ref, poolw_ref, pools_ref, bd_ref,
                 r_o, w_o, k_o, v_o, kkn_o, b_o, g_o, bonus_o, yp_o, ext_scr, *, tile, seq_tiles):
    halo = halo_ref[...]
    if seq_tiles:
        ti = pl.program_id(0) % seq_tiles
        halo = jnp.where(ti == 0, 0.0, halo)
    ext_scr[0:HALO, :] = halo
    ext_scr[HALO:HALO + tile, :] = z_ref[...]

    zr = z_ref[:, :D_SHIFT]
    zp = ext_scr[HALO - 1:HALO - 1 + tile, :D_SHIFT]
    zs = zr + (zp - zr) * mu_ref[...]
    r = zs[:, :D_RWKV]
    k = zs[:, D_RWKV:2 * D_RWKV]
    v = zs[:, 2 * D_RWKV:3 * D_RWKV]
    o = 3 * D_RWKV
    zw = zs[:, o:o + DECAY_LORA]
    za = zs[:, o + DECAY_LORA:o + DECAY_LORA + AAA_LORA]
    zg = zs[:, o + DECAY_LORA + AAA_LORA:]
    dec_in = w0_ref[...] + _bdot(jnp.tanh(zw), decb_ref[...])
    nx = -dec_in
    softplus = jnp.maximum(nx, 0.0) + jnp.log1p(jnp.exp(-jnp.abs(nx)))
    decay = jnp.exp(-jnp.exp(-softplus - 0.5))
    a = _sigmoid(a0_ref[...] + _bdot(za, ab_ref[...]))
    g = _bdot(_sigmoid(zg), gb_ref[...])
    bd = bd_ref[...]
    kkf = k * kk_ref[...]
    kkn = kkf / jnp.maximum(jnp.sqrt(_seg_sum(kkf * kkf, bd)), 1e-12)
    k2 = k * (1.0 + (a - 1.0) * ka_ref[...])
    bonus = _seg_sum(r * k2 * rk_ref[...], bd) * v
    r_o[...] = r.T
    w_o[...] = decay.T
    k_o[...] = k2.T
    v_o[...] = v
    kkn_o[...] = kkn.T
    b_o[...] = (kkn * a).T
    g_o[...] = g
    bonus_o[...] = bonus

    if seq_tiles:
        pos = ti * tile + lax.broadcasted_iota(jnp.int32, (tile, 1), 0)
    for gi, wdw in enumerate(POOL_WINDOWS):
        cols = slice(D_SHIFT + gi * POOL_GROUP, D_SHIFT + (gi + 1) * POOL_GROUP)
        u_ext = ext_scr[:, cols]
        s = u_ext
        shift = 1
        while shift < wdw:
            s = s + pltpu.roll(s, shift, 0)
            shift *= 2
        u = u_ext[HALO:, :]
        if seq_tiles:
            cnt = jnp.minimum(pos + 1, wdw).astype(F32)
        else:
            cnt = float(wdw)
        pooled = s[HALO:, :] / cnt - u
        y = _bdot(pooled, poolw_ref[gi])
        yp_o[:, gi * POOL_GROUP:(gi + 1) * POOL_GROUP] = y * pools_ref[:, gi * POOL_GROUP:(gi + 1) * POOL_GROUP]


def _premix(z, lw, bd, tile, seq_tiles):
    rows = z.shape[0]
    body = functools.partial(_premix_body, tile=tile, seq_tiles=seq_tiles)
    full = lambda shape: pl.BlockSpec(shape, lambda i: (0,) * len(shape))
    out_block = pl.BlockSpec((tile, D_RWKV), lambda i: (i, 0))
    out = jax.ShapeDtypeStruct((rows, D_RWKV), F32)
    out_block_t = pl.BlockSpec((D_RWKV, tile), lambda i: (0, i))
    out_t = jax.ShapeDtypeStruct((D_RWKV, rows), F32)
    transposed = (True, True, True, False, True, True, False, False, False)
    halo_blocks = tile // HALO
    return pl.pallas_call(
        body,
        grid=(rows // tile,),
        in_specs=[
            pl.BlockSpec((tile, D_IN), lambda i: (i, 0)),
            pl.BlockSpec((HALO, D_IN), lambda i: (jnp.maximum(i * halo_blocks - 1, 0), 0)),
            full((1, D_SHIFT)), full((1, D_RWKV)), full((DECAY_LORA, D_RWKV)), full((1, D_RWKV)),
            full((AAA_LORA, D_RWKV)), full((GATE_LORA, D_RWKV)), full((1, D_RWKV)), full((1, D_RWKV)),
            full((1, D_RWKV)), full((4, POOL_GROUP, POOL_GROUP)), full((1, D_POOL)), full((D_RWKV, D_RWKV)),
        ],
        out_specs=[out_block_t if t else out_block for t in transposed],
        out_shape=[out_t if t else out for t in transposed],
        scratch_shapes=[pltpu.VMEM((tile + HALO, D_IN), F32)],
        compiler_params=_cparams("parallel"),
        name="premix",
    )(z, z, lw["shift_mu"], lw["decay_w0"], lw["decay_b"], lw["a_0"], lw["a_b"], lw["g_b"], lw["k_k"],
      lw["k_a"], lw["r_k"], lw["pool_w"], lw["pool_scale"], bd)


GROUP_SEQS = LANES // (2 * N_RWKV_HEADS)
HALF_ROWS = RWKV_HEAD // 2
N_ACC = 4


def _tree_sum(parts):
    while len(parts) > 1:
        parts = [a + b for a, b in zip(parts[::2], parts[1::2])]
    return parts[0]


def _scan_body(w_ref, kk_ref, b_ref, k_ref, r_ref, v_ref, s0_ref, o_ref, sout_ref, st_scr, *, tc):
    c = pl.program_id(1)

    @pl.when(c == 0)
    def _():
        st_scr[...] = s0_ref[0]

    def step(t, carry):
        acc = [None] * N_ACC
        for j in range(RWKV_HEAD):
            term = st_scr[j] * kk_ref[0, t, pl.ds(j, 1), :]
            acc[j % N_ACC] = term if acc[j % N_ACC] is None else acc[j % N_ACC] + term
        sk = _tree_sum(acc)
        v = v_ref[0, t]
        acc = [None] * N_ACC
        for j in range(RWKV_HEAD):
            row_j = pl.ds(j, 1)
            s = (st_scr[j] * w_ref[0, t, row_j, :] - sk * b_ref[0, t, row_j, :]) + v * k_ref[0, t, row_j, :]
            st_scr[j] = s
            term = s * r_ref[0, t, row_j, :]
            acc[j % N_ACC] = term if acc[j % N_ACC] is None else acc[j % N_ACC] + term
        o_ref[0, t] = _tree_sum(acc)
        return carry

    if tc < SUBLANES:
        for t in range(tc):
            step(t, 0)
    else:
        lax.fori_loop(0, tc, step, 0)

    @pl.when(c == pl.num_programs(1) - 1)
    def _():
        sout_ref[0] = st_scr[...]


def _keys_to_lanes(xt):
    _, bsz, t_len = xt.shape
    y = xt.reshape(1, N_RWKV_HEADS, RWKV_HEAD, bsz // GROUP_SEQS, GROUP_SEQS, t_len)
    y = jnp.broadcast_to(y, (2, N_RWKV_HEADS, RWKV_HEAD, bsz // GROUP_SEQS, GROUP_SEQS, t_len))
    return y.transpose(3, 5, 2, 0, 4, 1).reshape(bsz // GROUP_SEQS, t_len, RWKV_HEAD, LANES)


def _values_to_lanes(x):
    bsz, t_len, _ = x.shape
    y = x.reshape(bsz // GROUP_SEQS, GROUP_SEQS, t_len, N_RWKV_HEADS, 2, HALF_ROWS).transpose(0, 2, 5, 4, 1, 3)
    return y.reshape(bsz // GROUP_SEQS, t_len, HALF_ROWS, LANES)


def _values_from_lanes(y):
    groups, t_len = y.shape[:2]
    y = y.reshape(groups, t_len, HALF_ROWS, 2, GROUP_SEQS, N_RWKV_HEADS).transpose(0, 4, 1, 5, 3, 2)
    return y.reshape(groups * GROUP_SEQS, t_len, D_RWKV)


def _state_to_lanes(s):
    bsz = s.shape[0]
    y = s.reshape(bsz // GROUP_SEQS, GROUP_SEQS, N_RWKV_HEADS, 2, HALF_ROWS, RWKV_HEAD).transpose(0, 5, 4, 3, 1, 2)
    return y.reshape(bsz // GROUP_SEQS, RWKV_HEAD, HALF_ROWS, LANES)


def _state_from_lanes(y):
    groups = y.shape[0]
    y = y.reshape(groups, RWKV_HEAD, HALF_ROWS, 2, GROUP_SEQS, N_RWKV_HEADS).transpose(0, 4, 5, 3, 2, 1)
    return y.reshape(groups * GROUP_SEQS, N_RWKV_HEADS, RWKV_HEAD, RWKV_HEAD)


def _scan(r, w, k, v, kk, b, s0, tc):
    bsz, t_len, _ = v.shape
    groups = bsz // GROUP_SEQS
    body = functools.partial(_scan_body, tc=tc)
    keys = pl.BlockSpec((1, tc, RWKV_HEAD, LANES), lambda i, c: (i, c, 0, 0))
    vals = pl.BlockSpec((1, tc, HALF_ROWS, LANES), lambda i, c: (i, c, 0, 0))
    st = pl.BlockSpec((1, RWKV_HEAD, HALF_ROWS, LANES), lambda i, c: (i, 0, 0, 0))
    o, s_fin = pl.pallas_call(
        body,
        grid=(groups, t_len // tc),
        in_specs=[keys, keys, keys, keys, keys, vals, st],
        out_specs=[vals, st],
        out_shape=[jax.ShapeDtypeStruct((groups, t_len, HALF_ROWS, LANES), F32),
                   jax.ShapeDtypeStruct((groups, RWKV_HEAD, HALF_ROWS, LANES), F32)],
        scratch_shapes=[pltpu.VMEM((RWKV_HEAD, HALF_ROWS, LANES), F32)],
        compiler_params=_cparams("parallel", "arbitrary"),
        name="scan",
    )(_keys_to_lanes(w), _keys_to_lanes(kk), _keys_to_lanes(b), _keys_to_lanes(k), _keys_to_lanes(r),
      _values_to_lanes(v), _state_to_lanes(s0))
    return _values_from_lanes(o), _state_from_lanes(s_fin)


def _postmix_body(x_ref, o_ref, bonus_ref, g_ref, yp_ref, lng_ref, lnb_ref, wout_ref, nfg_ref, bd_ref,
                  h_o, xnt_o):
    bd = bd_ref[...]
    o = o_ref[...]
    inv_n = 1.0 / RWKV_HEAD
    d = o - _seg_sum(o, bd) * inv_n
    var = _seg_sum(d * d, bd) * inv_n
    y_r = (d * lax.rsqrt(var + LNX_EPS) * lng_ref[...] + lnb_ref[...] + bonus_ref[...]) * g_ref[...]
    h = x_ref[...] + _bdot(y_r, wout_ref[:D_RWKV, :]) + _bdot(yp_ref[...], wout_ref[D_RWKV:, :])
    h_o[...] = h
    xnt_o[...] = _rms(h, nfg_ref[...]).T.astype(BF16)


def _postmix(x2d, o, bonus, g, yp, lw, w_out_bf16, bd, tile):
    rows = x2d.shape[0]
    full = lambda shape: pl.BlockSpec(shape, lambda i: (0,) * len(shape))
    half = pl.BlockSpec((tile, D_RWKV), lambda i: (i, 0))
    wide = pl.BlockSpec((tile, D_MODEL), lambda i: (i, 0))
    return pl.pallas_call(
        _postmix_body,
        grid=(rows // tile,),
        in_specs=[wide, half, half, half, half, full((1, D_RWKV)), full((1, D_RWKV)),
                  full((D_MODEL, D_MODEL)), full((1, D_MODEL)), full((D_RWKV, D_RWKV))],
        out_specs=[wide, pl.BlockSpec((D_MODEL, tile), lambda i: (0, i))],
        out_shape=[jax.ShapeDtypeStruct((rows, D_MODEL), F32), jax.ShapeDtypeStruct((D_MODEL, rows), BF16)],
        compiler_params=_cparams("parallel"),
        name="postmix",
    )(x2d, o, bonus, g, yp, lw["lnx_g"], lw["lnx_b"], w_out_bf16, lw["norm_ffn_g"], bd)


def _oddeven_mergesort_pairs(n):
    pairs = []
    p = 1
    while p < n:
        k = p
        while k >= 1:
            for j in range(k % p, n - k, 2 * k):
                for i in range(min(k, n - j - k)):
                    if (i + j) // (2 * p) == (i + j + k) // (2 * p):
                        pairs.append((i + j, i + j + k))
            k //= 2
        p *= 2
    return pairs


_SORT16 = _oddeven_mergesort_pairs(PEER_TOPK)


def _exchange(lst, i, j):
    lst[i], lst[j] = jnp.maximum(lst[i], lst[j]), jnp.minimum(lst[i], lst[j])


def _merge_top16(a, b):
    c = [jnp.maximum(a[i], b[PEER_TOPK - 1 - i]) for i in range(PEER_TOPK)]
    stride = PEER_TOPK // 2
    while stride:
        for i in range(PEER_TOPK):
            if not i & stride:
                _exchange(c, i, i + stride)
        stride //= 2
    return c


def _top16_of_keys(s):
    lst = [s[SUBLANES * g:SUBLANES * (g + 1), :] for g in range(N_KEYS // SUBLANES)]
    for i, j in _SORT16:
        _exchange(lst, i, j)
    shift = SUBLANES // 2
    while shift:
        lst = _merge_top16(lst, [pltpu.roll(x, shift, 0) for x in lst])
        shift //= 2
    return lst


def _route_body(xnt_ref, wqt_ref, keys_ref, s1_o, s2_o, e1_o, e2_o, *, tile):
    qt = jnp.dot(wqt_ref[...], xnt_ref[...], preferred_element_type=F32).astype(BF16)
    head = lax.broadcasted_iota(jnp.int32, (PEER_HEADS, tile), 0)
    neg = jnp.full((PEER_HEADS, tile), NEG_INF, F32)
    top = [[neg] * PEER_TOPK, [neg] * PEER_TOPK]
    for h in range(PEER_HEADS):
        base = h * PEER_DK
        for half, s_o in enumerate((s1_o, s2_o)):
            q_half = qt[base + half * PEER_DK_HALF:base + (half + 1) * PEER_DK_HALF]
            s = jnp.dot(keys_ref[half], q_half, preferred_element_type=F32)
            s_o[h] = s
            best = _top16_of_keys(s)
            top[half] = [jnp.where(head == h, best[a], top[half][a]) for a in range(PEER_TOPK)]
    v1, v2 = top
    pad = lambda lst: lst + [neg] * (PEER_TOPK - len(lst))
    sc = [v1[0] + v2[b] for b in range(PEER_TOPK)]
    for a in range(1, PEER_TOPK // 2):
        sc = _merge_top16(sc, pad([v1[a] + v2[b] for b in range(PEER_TOPK // (a + 1))]))
    sc = _merge_top16(sc, pad([v1[a] + v2[0] for a in range(PEER_TOPK // 2, PEER_TOPK)]))
    z = jnp.ones_like(sc[0])
    for kth in sc[1:]:
        z = z + jnp.exp(kth - sc[0])
    tau = sc[PEER_TOPK - 1]
    for h in range(PEER_HEADS):
        row = slice(h, h + 1)
        s1 = s1_o[h]
        e1_o[h] = jnp.exp(s1 - v1[0][row])
        e2_o[h] = jnp.exp(s2_o[h] - v2[0][row]) / z[row]
        s1_o[h] = tau[row] - s1


def _route(xnt, wqt_bf16, keys_bf16, tile):
    rows = xnt.shape[1]
    body = functools.partial(_route_body, tile=tile)
    sco = pl.BlockSpec((PEER_HEADS, N_KEYS, tile), lambda i: (0, 0, i))
    sc_shape = jax.ShapeDtypeStruct((PEER_HEADS, N_KEYS, rows), F32)
    return pl.pallas_call(
        body,
        grid=(rows // tile,),
        in_specs=[
            pl.BlockSpec((D_MODEL, tile), lambda i: (0, i)),
            pl.BlockSpec((PEER_HEADS * PEER_DK, D_MODEL), lambda i: (0, 0)),
            pl.BlockSpec((2, N_KEYS, PEER_DK_HALF), lambda i: (0, 0, 0)),
        ],
        out_specs=[sco, sco, sco, sco],
        out_shape=[sc_shape, sc_shape, sc_shape, sc_shape],
        compiler_params=_cparams("parallel"),
        name="route",
    )(xnt, wqt_bf16, keys_bf16)


E1_PER_CHUNK = 8
CHUNK = E1_PER_CHUNK * N_KEYS


def _gelu(x):
    return 0.5 * x * (1.0 + lax.erf(x * (1.0 / math.sqrt(2.0))))


def _peer_body(xnt_ref, u_ref, vt_ref, c1_ref, s2_ref, e1_ref, e2_ref, out_o, acc_scr, a_scr, *, tile):
    c = pl.program_id(1)

    @pl.when(c == 0)
    def _():
        acc_scr[...] = jnp.zeros_like(acc_scr)

    ht = jnp.dot(u_ref[...], xnt_ref[...], preferred_element_type=F32)
    for e in range(E1_PER_CHUNK):
        e1 = c * E1_PER_CHUNK + e
        wgt = None
        for h in range(PEER_HEADS):
            c1row = c1_ref[h, pl.ds(e1, 1), :]
            e1row = e1_ref[h, pl.ds(e1, 1), :]
            term = jnp.where(s2_ref[h] >= c1row, e1row * e2_ref[h], 0.0)
            wgt = term if wgt is None else wgt + term
        rows = slice(e * N_KEYS, (e + 1) * N_KEYS)
        a_scr[rows, :] = (_gelu(ht[rows, :]) * wgt).astype(BF16)
    acc_scr[...] += jnp.dot(vt_ref[...], a_scr[...], preferred_element_type=F32)

    @pl.when(c == pl.num_programs(1) - 1)
    def _():
        out_o[...] = acc_scr[...].T


def _peer(xnt, u_bf16, vt_bf16, c1, s2, e1, e2, tile):
    rows = xnt.shape[1]
    body = functools.partial(_peer_body, tile=tile)
    sco = pl.BlockSpec((PEER_HEADS, N_KEYS, tile), lambda i, c: (0, 0, i))
    return pl.pallas_call(
        body,
        grid=(rows // tile, N_EXPERTS // CHUNK),
        in_specs=[
            pl.BlockSpec((D_MODEL, tile), lambda i, c: (0, i)),
            pl.BlockSpec((CHUNK, D_MODEL), lambda i, c: (c, 0)),
            pl.BlockSpec((D_MODEL, CHUNK), lambda i, c: (0, c)),
            sco, sco, sco, sco,
        ],
        out_specs=pl.BlockSpec((tile, D_MODEL), lambda i, c: (i, 0)),
        out_shape=jax.ShapeDtypeStruct((rows, D_MODEL), F32),
        scratch_shapes=[pltpu.VMEM((D_MODEL, tile), F32), pltpu.VMEM((CHUNK, tile), BF16)],
        compiler_params=_cparams("parallel", "arbitrary"),
        name="peer",
    )(xnt, u_bf16, vt_bf16, c1, s2, e1, e2)


def _final_body(h_ref, peer_ref, p_ref, npg_ref, gatew_ref, plew_ref, fg_ref, y_o):
    h = h_ref[...] + peer_ref[...]
    gate = _sigmoid(_bdot(_rms(h, npg_ref[...]), gatew_ref[...]))
    h = h + _bdot(p_ref[...], plew_ref[...]) * gate
    y_o[...] = _rms(h, fg_ref[...])


def _final(h, peer, p2d, lw, gate_w_bf16, ple_w_bf16, final_g, tile):
    rows = h.shape[0]
    full = lambda shape: pl.BlockSpec(shape, lambda i: (0,) * len(shape))
    wide = pl.BlockSpec((tile, D_MODEL), lambda i: (i, 0))
    return pl.pallas_call(
        _final_body,
        grid=(rows // tile,),
        in_specs=[wide, wide, pl.BlockSpec((tile, PLE_DIM), lambda i: (i, 0)), full((1, D_MODEL)),
                  full((D_MODEL, D_MODEL)), full((PLE_DIM, D_MODEL)), full((1, D_MODEL))],
        out_specs=wide,
        out_shape=jax.ShapeDtypeStruct((rows, D_MODEL), F32),
        compiler_params=_cparams("parallel"),
        name="final",
    )(h, peer, p2d, lw["norm_ple_g"], gate_w_bf16, ple_w_bf16, final_g)


def _block_diag_ones():
    seg = jnp.arange(D_RWKV) // RWKV_HEAD
    return (seg[:, None] == seg[None, :]).astype(BF16)


def _tail(x2d, p2d, o, bonus, g, yp, lw, wts, bd, final_g, tile):
    h, xnt = _postmix(x2d, o, bonus, g, yp, lw, wts["w_out"], bd, tile)
    c1, s2, e1, e2 = _route(xnt, wts["wqt"], wts["keys"], tile)
    peer = _peer(xnt, wts["u"], wts["vt"], c1, s2, e1, e2, tile)
    return _final(h, peer, p2d, lw, wts["gate_w"], wts["ple_w"], final_g, tile)


def kernel(x_prompt, x_sample, state_shift, state_wkv, state_pool, p_prompt, p_sample, norm_mix_g, w_in, shift_mu, decay_w0, decay_b, a_0, a_b, g_b, k_k, k_a, r_k, lnx_g, lnx_b, pool_w, pool_scale, w_out, norm_ffn_g, peer_wq, peer_keys, peer_u, peer_v, norm_ple_g, ple_w, ple_gate_w, final_norm_g):
    assert norm_mix_g.shape[0] == 1, "single trunk layer"
    bsz, seq, _ = x_prompt.shape
    dbsz, dseq, _ = x_sample.shape
    row = lambda a: a.reshape(1, -1)
    lw = dict(
        shift_mu=row(shift_mu[0]), decay_w0=row(decay_w0[0]), decay_b=decay_b[0], a_0=row(a_0[0]), a_b=a_b[0],
        g_b=g_b[0], k_k=row(k_k[0]), k_a=row(k_a[0]), r_k=row(r_k[0]), lnx_g=row(lnx_g[0]), lnx_b=row(lnx_b[0]),
        pool_w=pool_w[0], pool_scale=row(pool_scale[0]), norm_ffn_g=row(norm_ffn_g[0]),
        norm_ple_g=row(norm_ple_g[0]))
    wts = dict(
        w_out=w_out[0].astype(BF16), wqt=peer_wq[0].T.astype(BF16), keys=peer_keys[0].astype(BF16),
        u=peer_u[0].astype(BF16), vt=peer_v[0].T.astype(BF16), gate_w=ple_gate_w[0].astype(BF16),
        ple_w=ple_w[0].astype(BF16))
    w_in_bf16 = w_in[0].astype(BF16)
    mix_g = row(norm_mix_g[0])
    final_g = row(final_norm_g)
    bd = _block_diag_ones()
    tile = 512

    xp = x_prompt.reshape(bsz * seq, D_MODEL)
    zp = _proj(xp, mix_g, w_in_bf16, tile)
    pm_tile = 256
    r, w, k, v, kk, b, g, bonus, yp = _premix(zp, lw, bd, pm_tile, seq // pm_tile)
    seq3 = lambda a: a.reshape(D_RWKV, bsz, seq)
    s0 = jnp.zeros((bsz, N_RWKV_HEADS, RWKV_HEAD, RWKV_HEAD), F32)
    o, s_fin = _scan(seq3(r), seq3(w), seq3(k), v.reshape(bsz, seq, D_RWKV), seq3(kk), seq3(b), s0, 64)
    y_prompt = _tail(xp, p_prompt[0].reshape(bsz * seq, PLE_DIM), o.reshape(bsz * seq, D_RWKV), bonus, g, yp,
                     lw, wts, bd, final_g, tile).reshape(bsz, seq, D_MODEL)
    zp3 = zp.reshape(bsz, seq, D_IN)
    shift_prompt = zp3[:, -1, :D_SHIFT][None]
    wkv_prompt = s_fin[None]
    pool_prompt = zp3[:, seq - POOL_BUF:, D_SHIFT:][None]

    xs = x_sample.reshape(dbsz * dseq, D_MODEL)
    zs = _proj(xs, mix_g, w_in_bf16, dbsz * dseq).reshape(dbsz, dseq, D_IN)
    hist = jnp.zeros((dbsz, HALO, D_IN), F32)
    hist = hist.at[:, HALO - POOL_BUF:, D_SHIFT:].set(state_pool[0])
    hist = hist.at[:, HALO - 1, :D_SHIFT].set(state_shift[0])
    ext_len = HALO + dseq
    z_ext = jnp.concatenate([hist, zs], axis=1).reshape(dbsz * ext_len, D_IN)
    outs = _premix(z_ext, lw, bd, 512, 0)
    keep = lambda a: a.reshape(dbsz, ext_len, D_RWKV)[:, HALO:, :]
    keep_t = lambda a: a.reshape(D_RWKV, dbsz, ext_len)[:, :, HALO:]
    r, w, k, kk, b = [keep_t(outs[i]) for i in (0, 1, 2, 4, 5)]
    v, g, bonus, yp = [keep(outs[i]) for i in (3, 6, 7, 8)]
    o, s_fin = _scan(r, w, k, v, kk, b, state_wkv[0], dseq)
    flat = lambda a: a.reshape(dbsz * dseq, D_RWKV)
    y_sample = _tail(xs, p_sample[0].reshape(dbsz * dseq, PLE_DIM), flat(o), flat(bonus), flat(g), flat(yp),
                     lw, wts, bd, final_g, tile).reshape(dbsz, dseq, D_MODEL)
    shift_sample = zs[:, -1, :D_SHIFT][None]
    wkv_sample = s_fin[None]
    pool_sample = jnp.concatenate([state_pool[0], zs[:, :, D_SHIFT:]], axis=1)[:, -POOL_BUF:][None]

    return (y_prompt, y_sample, shift_prompt, wkv_prompt, pool_prompt, shift_sample, wkv_sample, pool_sample)
```

```python
import functools
import math

import jax
import jax.numpy as jnp
from jax import lax
from jax.experimental import pallas as pl
from jax.experimental.pallas import tpu as pltpu

F32 = jnp.float32
BF16 = jnp.bfloat16

D_MODEL = 1024
RWKV_HEAD = 64
D_RWKV = 512
N_RWKV_HEADS = 8
DECAY_LORA = 64
AAA_LORA = 64
GATE_LORA = 128
D_SHIFT = 3 * D_RWKV + DECAY_LORA + AAA_LORA + GATE_LORA
D_POOL = 512
POOL_WINDOWS = (2, 4, 8, 16)
POOL_GROUP = 128
POOL_BUF = 15
D_IN = D_SHIFT + D_POOL
N_KEYS = 128
N_EXPERTS = N_KEYS * N_KEYS
PEER_HEADS = 8
PEER_TOPK = 16
PEER_DK = 256
PEER_DK_HALF = 128
PLE_DIM = 256
NORM_EPS = 1e-6
LNX_EPS = 64e-5
PAST_LEN = 16384

HALO = 16
LANES = 128
SUBLANES = 8
VMEM_LIMIT = 48 * 1024 * 1024

NEG_INF = float("-inf")


def _cparams(*sem):
    return pltpu.CompilerParams(dimension_semantics=sem, vmem_limit_bytes=VMEM_LIMIT)


def _rms(x, g):
    return x * lax.rsqrt(jnp.mean(x * x, axis=-1, keepdims=True) + NORM_EPS) * g


def _sigmoid(x):
    return 1.0 / (1.0 + jnp.exp(-x))


def _bdot(a, b):
    return jnp.dot(a.astype(BF16), b.astype(BF16), preferred_element_type=F32)


def _seg_sum(x, bd):
    x1 = x.astype(BF16)
    r1 = x - x1.astype(F32)
    x2 = r1.astype(BF16)
    x3 = (r1 - x2.astype(F32)).astype(BF16)
    dot = functools.partial(jnp.dot, preferred_element_type=F32)
    return dot(x1, bd) + dot(x2, bd) + dot(x3, bd)


def _proj_body(x_ref, g_ref, w_ref, z_ref):
    z_ref[...] = _bdot(_rms(x_ref[...], g_ref[...]), w_ref[...])


def _proj(x2d, g, w_bf16, tile):
    rows = x2d.shape[0]
    return pl.pallas_call(
        _proj_body,
        grid=(rows // tile,),
        in_specs=[
            pl.BlockSpec((tile, D_MODEL), lambda i: (i, 0)),
            pl.BlockSpec((1, D_MODEL), lambda i: (0, 0)),
            pl.BlockSpec((D_MODEL, D_IN), lambda i: (0, 0)),
        ],
        out_specs=pl.BlockSpec((tile, D_IN), lambda i: (i, 0)),
        out_shape=jax.ShapeDtypeStruct((rows, D_IN), F32),
        compiler_params=_cparams("parallel"),
        name="proj",
    )(x2d, g, w_bf16)


def _premix_body(z_ref, halo_ref, mu_ref, w0_ref, decb_ref, a0_ref, ab_ref, gb_ref, kk_ref, ka_ref,
                 rk_ref, poolw_ref, pools_ref, bd_ref,
                 r_o, w_o, k_o, v_o, kkn_o, b_o, g_o, bonus_o, yp_o, ext_scr, *, tile, seq_tiles):
    halo = halo_ref[...]
    if seq_tiles:
        ti = pl.program_id(0) % seq_tiles
        halo = jnp.where(ti == 0, 0.0, halo)
    ext_scr[0:HALO, :] = halo
    ext_scr[HALO:HALO + tile, :] = z_ref[...]

    zr = z_ref[:, :D_SHIFT]
    zp = ext_scr[HALO - 1:HALO - 1 + tile, :D_SHIFT]
    zs = zr + (zp - zr) * mu_ref[...]
    r = zs[:, :D_RWKV]
    k = zs[:, D_RWKV:2 * D_RWKV]
    v = zs[:, 2 * D_RWKV:3 * D_RWKV]
    o = 3 * D_RWKV
    zw = zs[:, o:o + DECAY_LORA]
    za = zs[:, o + DECAY_LORA:o + DECAY_LORA + AAA_LORA]
    zg = zs[:, o + DECAY_LORA + AAA_LORA:]
    dec_in = w0_ref[...] + _bdot(jnp.tanh(zw), decb_ref[...])
    nx = -dec_in
    softplus = jnp.maximum(nx, 0.0) + jnp.log1p(jnp.exp(-jnp.abs(nx)))
    decay = jnp.exp(-jnp.exp(-softplus - 0.5))
    a = _sigmoid(a0_ref[...] + _bdot(za, ab_ref[...]))
    g = _bdot(_sigmoid(zg), gb_ref[...])
    bd = bd_ref[...]
    kkf = k * kk_ref[...]
    kkn = kkf / jnp.maximum(jnp.sqrt(_seg_sum(kkf * kkf, bd)), 1e-12)
    k2 = k * (1.0 + (a - 1.0) * ka_ref[...])
    bonus = _seg_sum(r * k2 * rk_ref[...], bd) * v
    r_o[...] = r.T
    w_o[...] = decay.T
    k_o[...] = k2.T
    v_o[...] = v.T
    kkn_o[...] = kkn.T
    b_o[...] = (kkn * a).T
    g_o[...] = g
    bonus_o[...] = bonus

    if seq_tiles:
        pos = ti * tile + lax.broadcasted_iota(jnp.int32, (tile, 1), 0)
    for gi, wdw in enumerate(POOL_WINDOWS):
        cols = slice(D_SHIFT + gi * POOL_GROUP, D_SHIFT + (gi + 1) * POOL_GROUP)
        u_ext = ext_scr[:, cols]
        s = u_ext
        shift = 1
        while shift < wdw:
            s = s + pltpu.roll(s, shift, 0)
            shift *= 2
        u = u_ext[HALO:, :]
        if seq_tiles:
            cnt = jnp.minimum(pos + 1, wdw).astype(F32)
        else:
            cnt = float(wdw)
        pooled = s[HALO:, :] / cnt - u
        y = _bdot(pooled, poolw_ref[gi])
        yp_o[:, gi * POOL_GROUP:(gi + 1) * POOL_GROUP] = y * pools_ref[:, gi * POOL_GROUP:(gi + 1) * POOL_GROUP]


def _premix(z, lw, bd, tile, seq_tiles):
    rows = z.shape[0]
    body = functools.partial(_premix_body, tile=tile, seq_tiles=seq_tiles)
    full = lambda shape: pl.BlockSpec(shape, lambda i: (0,) * len(shape))
    out_block = pl.BlockSpec((tile, D_RWKV), lambda i: (i, 0))
    out = jax.ShapeDtypeStruct((rows, D_RWKV), F32)
    out_block_t = pl.BlockSpec((D_RWKV, tile), lambda i: (0, i))
    out_t = jax.ShapeDtypeStruct((D_RWKV, rows), F32)
    transposed = (True, True, True, True, True, True, False, False, False)
    halo_blocks = tile // HALO
    return pl.pallas_call(
        body,
        grid=(rows // tile,),
        in_specs=[
            pl.BlockSpec((tile, D_IN), lambda i: (i, 0)),
            pl.BlockSpec((HALO, D_IN), lambda i: (jnp.maximum(i * halo_blocks - 1, 0), 0)),
            full((1, D_SHIFT)), full((1, D_RWKV)), full((DECAY_LORA, D_RWKV)), full((1, D_RWKV)),
            full((AAA_LORA, D_RWKV)), full((GATE_LORA, D_RWKV)), full((1, D_RWKV)), full((1, D_RWKV)),
            full((1, D_RWKV)), full((4, POOL_GROUP, POOL_GROUP)), full((1, D_POOL)), full((D_RWKV, D_RWKV)),
        ],
        out_specs=[out_block_t if t else out_block for t in transposed],
        out_shape=[out_t if t else out for t in transposed],
        scratch_shapes=[pltpu.VMEM((tile + HALO, D_IN), F32)],
        compiler_params=_cparams("parallel"),
        name="premix",
    )(z, z, lw["shift_mu"], lw["decay_w0"], lw["decay_b"], lw["a_0"], lw["a_b"], lw["g_b"], lw["k_k"],
      lw["k_a"], lw["r_k"], lw["pool_w"], lw["pool_scale"], bd)


GROUP_SEQS = LANES // (2 * N_RWKV_HEADS)
HALF_ROWS = RWKV_HEAD // 2
N_ACC = 4


def _tree_sum(parts):
    while len(parts) > 1:
        parts = [a + b for a, b in zip(parts[::2], parts[1::2])]
    return parts[0]


def _scan_body(w_ref, kk_ref, b_ref, k_ref, r_ref, v_ref, s0_ref, o_ref, sout_ref, st_scr, *, tc):
    c = pl.program_id(1)

    @pl.when(c == 0)
    def _():
        st_scr[...] = s0_ref[0]

    def step(t, carry):
        acc = [None] * N_ACC
        for j in range(RWKV_HEAD):
            term = st_scr[j] * kk_ref[0, t, pl.ds(j, 1), :]
            acc[j % N_ACC] = term if acc[j % N_ACC] is None else acc[j % N_ACC] + term
        sk = _tree_sum(acc)
        v = v_ref[0, t]
        acc = [None] * N_ACC
        for j in range(RWKV_HEAD):
            row_j = pl.ds(j, 1)
            s = (st_scr[j] * w_ref[0, t, row_j, :] - sk * b_ref[0, t, row_j, :]) + v * k_ref[0, t, row_j, :]
            st_scr[j] = s
            term = s * r_ref[0, t, row_j, :]
            acc[j % N_ACC] = term if acc[j % N_ACC] is None else acc[j % N_ACC] + term
        o_ref[0, t] = _tree_sum(acc)
        return carry

    if tc < SUBLANES:
        for t in range(tc):
            step(t, 0)
    else:
        lax.fori_loop(0, tc, step, 0)

    @pl.when(c == pl.num_programs(1) - 1)
    def _():
        sout_ref[0] = st_scr[...]


def _keys_to_lanes(xt):
    _, bsz, t_len = xt.shape
    y = xt.reshape(1, N_RWKV_HEADS, RWKV_HEAD, bsz // GROUP_SEQS, GROUP_SEQS, t_len)
    y = jnp.broadcast_to(y, (2, N_RWKV_HEADS, RWKV_HEAD, bsz // GROUP_SEQS, GROUP_SEQS, t_len))
    return y.transpose(3, 5, 2, 0, 4, 1).reshape(bsz // GROUP_SEQS, t_len, RWKV_HEAD, LANES)


def _values_to_lanes(xt):
    _, bsz, t_len = xt.shape
    y = xt.reshape(N_RWKV_HEADS, 2, HALF_ROWS, bsz // GROUP_SEQS, GROUP_SEQS, t_len).transpose(3, 5, 2, 1, 4, 0)
    return y.reshape(bsz // GROUP_SEQS, t_len, HALF_ROWS, LANES)


def _values_from_lanes(y):
    groups, t_len = y.shape[:2]
    y = y.reshape(groups, t_len, HALF_ROWS, 2, GROUP_SEQS, N_RWKV_HEADS).transpose(5, 3, 2, 0, 4, 1)
    return y.reshape(D_RWKV, groups * GROUP_SEQS * t_len)


def _state_to_lanes(s):
    bsz = s.shape[0]
    y = s.reshape(bsz // GROUP_SEQS, GROUP_SEQS, N_RWKV_HEADS, 2, HALF_ROWS, RWKV_HEAD).transpose(0, 5, 4, 3, 1, 2)
    return y.reshape(bsz // GROUP_SEQS, RWKV_HEAD, HALF_ROWS, LANES)


def _state_from_lanes(y):
    groups = y.shape[0]
    y = y.reshape(groups, RWKV_HEAD, HALF_ROWS, 2, GROUP_SEQS, N_RWKV_HEADS).transpose(0, 4, 5, 3, 2, 1)
    return y.reshape(groups * GROUP_SEQS, N_RWKV_HEADS, RWKV_HEAD, RWKV_HEAD)


def _scan(r, w, k, v, kk, b, s0, tc):
    _, bsz, t_len = v.shape
    groups = bsz // GROUP_SEQS
    body = functools.partial(_scan_body, tc=tc)
    keys = pl.BlockSpec((1, tc, RWKV_HEAD, LANES), lambda i, c: (i, c, 0, 0))
    vals = pl.BlockSpec((1, tc, HALF_ROWS, LANES), lambda i, c: (i, c, 0, 0))
    st = pl.BlockSpec((1, RWKV_HEAD, HALF_ROWS, LANES), lambda i, c: (i, 0, 0, 0))
    o, s_fin = pl.pallas_call(
        body,
        grid=(groups, t_len // tc),
        in_specs=[keys, keys, keys, keys, keys, vals, st],
        out_specs=[vals, st],
        out_shape=[jax.ShapeDtypeStruct((groups, t_len, HALF_ROWS, LANES), F32),
                   jax.ShapeDtypeStruct((groups, RWKV_HEAD, HALF_ROWS, LANES), F32)],
        scratch_shapes=[pltpu.VMEM((RWKV_HEAD, HALF_ROWS, LANES), F32)],
        compiler_params=_cparams("parallel", "arbitrary"),
        name="scan",
    )(_keys_to_lanes(w), _keys_to_lanes(kk), _keys_to_lanes(b), _keys_to_lanes(k), _keys_to_lanes(r),
      _values_to_lanes(v), _state_to_lanes(s0))
    return _values_from_lanes(o), _state_from_lanes(s_fin)


def _postmix_body(x_ref, o_ref, bonus_ref, g_ref, yp_ref, lng_ref, lnb_ref, wout_ref, nfg_ref, bd_ref,
                  h_o, xnt_o):
    bd = bd_ref[...]
    o = o_ref[...].T
    inv_n = 1.0 / RWKV_HEAD
    d = o - _seg_sum(o, bd) * inv_n
    var = _seg_sum(d * d, bd) * inv_n
    y_r = (d * lax.rsqrt(var + LNX_EPS) * lng_ref[...] + lnb_ref[...] + bonus_ref[...]) * g_ref[...]
    h = x_ref[...] + _bdot(y_r, wout_ref[:D_RWKV, :]) + _bdot(yp_ref[...], wout_ref[D_RWKV:, :])
    h_o[...] = h
    xnt_o[...] = _rms(h, nfg_ref[...]).T.astype(BF16)


def _postmix(x2d, o, bonus, g, yp, lw, w_out_bf16, bd, tile):
    rows = x2d.shape[0]
    full = lambda shape: pl.BlockSpec(shape, lambda i: (0,) * len(shape))
    half = pl.BlockSpec((tile, D_RWKV), lambda i: (i, 0))
    wide = pl.BlockSpec((tile, D_MODEL), lambda i: (i, 0))
    return pl.pallas_call(
        _postmix_body,
        grid=(rows // tile,),
        in_specs=[wide, pl.BlockSpec((D_RWKV, tile), lambda i: (0, i)), half, half, half,
                  full((1, D_RWKV)), full((1, D_RWKV)),
                  full((D_MODEL, D_MODEL)), full((1, D_MODEL)), full((D_RWKV, D_RWKV))],
        out_specs=[wide, pl.BlockSpec((D_MODEL, tile), lambda i: (0, i))],
        out_shape=[jax.ShapeDtypeStruct((rows, D_MODEL), F32), jax.ShapeDtypeStruct((D_MODEL, rows), BF16)],
        compiler_params=_cparams("parallel"),
        name="postmix",
    )(x2d, o, bonus, g, yp, lw["lnx_g"], lw["lnx_b"], w_out_bf16, lw["norm_ffn_g"], bd)


def _oddeven_mergesort_pairs(n):
    pairs = []
    p = 1
    while p < n:
        k = p
        while k >= 1:
            for j in range(k % p, n - k, 2 * k):
                for i in range(min(k, n - j - k)):
                    if (i + j) // (2 * p) == (i + j + k) // (2 * p):
                        pairs.append((i + j, i + j + k))
            k //= 2
        p *= 2
    return pairs


_SORT16 = _oddeven_mergesort_pairs(PEER_TOPK)


def _exchange(lst, i, j):
    lst[i], lst[j] = jnp.maximum(lst[i], lst[j]), jnp.minimum(lst[i], lst[j])


def _merge_top16(a, b):
    c = [jnp.maximum(a[i], b[PEER_TOPK - 1 - i]) for i in range(PEER_TOPK)]
    stride = PEER_TOPK // 2
    while stride:
        for i in range(PEER_TOPK):
            if not i & stride:
                _exchange(c, i, i + stride)
        stride //= 2
    return c


def _top16_of_keys(s):
    lst = [s[SUBLANES * g:SUBLANES * (g + 1), :] for g in range(N_KEYS // SUBLANES)]
    for i, j in _SORT16:
        _exchange(lst, i, j)
    shift = SUBLANES // 2
    while shift:
        lst = _merge_top16(lst, [pltpu.roll(x, shift, 0) for x in lst])
        shift //= 2
    return lst


def _route_body(xnt_ref, wqt_ref, keys_ref, s1_o, s2_o, c2_o, *, tile):
    qt = jnp.dot(wqt_ref[...], xnt_ref[...], preferred_element_type=F32).astype(BF16)
    head = lax.broadcasted_iota(jnp.int32, (PEER_HEADS, tile), 0)
    neg = jnp.full((PEER_HEADS, tile), NEG_INF, F32)
    top = [[neg] * PEER_TOPK, [neg] * PEER_TOPK]
    for h in range(PEER_HEADS):
        base = h * PEER_DK
        for half, s_o in enumerate((s1_o, s2_o)):
            q_half = qt[base + half * PEER_DK_HALF:base + (half + 1) * PEER_DK_HALF]
            s = jnp.dot(keys_ref[half], q_half, preferred_element_type=F32)
            s_o[h] = s
            best = _top16_of_keys(s)
            top[half] = [jnp.where(head == h, best[a], top[half][a]) for a in range(PEER_TOPK)]
    v1, v2 = top
    pad = lambda lst: lst + [neg] * (PEER_TOPK - len(lst))
    sc = [v1[0] + v2[b] for b in range(PEER_TOPK)]
    for a in range(1, PEER_TOPK // 2):
        sc = _merge_top16(sc, pad([v1[a] + v2[b] for b in range(PEER_TOPK // (a + 1))]))
    sc = _merge_top16(sc, pad([v1[a] + v2[0] for a in range(PEER_TOPK // 2, PEER_TOPK)]))
    z = jnp.ones_like(sc[0])
    for kth in sc[1:]:
        z = z + jnp.exp(kth - sc[0])
    tau = sc[PEER_TOPK - 1]
    lse = sc[0] + jnp.log(z)
    for h in range(PEER_HEADS):
        row = slice(h, h + 1)
        s1 = s1_o[h]
        c2_o[h] = lse[row] - s1
        s1_o[h] = tau[row] - s1


def _route(xnt, wqt_bf16, keys_bf16, tile):
    rows = xnt.shape[1]
    body = functools.partial(_route_body, tile=tile)
    sco = pl.BlockSpec((PEER_HEADS, N_KEYS, tile), lambda i: (0, 0, i))
    sc_shape = jax.ShapeDtypeStruct((PEER_HEADS, N_KEYS, rows), F32)
    return pl.pallas_call(
        body,
        grid=(rows // tile,),
        in_specs=[
            pl.BlockSpec((D_MODEL, tile), lambda i: (0, i)),
            pl.BlockSpec((PEER_HEADS * PEER_DK, D_MODEL), lambda i: (0, 0)),
            pl.BlockSpec((2, N_KEYS, PEER_DK_HALF), lambda i: (0, 0, 0)),
        ],
        out_specs=[sco, sco, sco],
        out_shape=[sc_shape, sc_shape, sc_shape],
        compiler_params=_cparams("parallel"),
        name="route",
    )(xnt, wqt_bf16, keys_bf16)


E1_PER_CHUNK = 8
CHUNK = E1_PER_CHUNK * N_KEYS


def _gelu(x):
    return 0.5 * x * (1.0 + lax.erf(x * (1.0 / math.sqrt(2.0))))


def _peer_body(xnt_ref, u_ref, vt_ref, c1_ref, s2_ref, c2_ref, out_o, acc_scr, a_scr, *, tile):
    c = pl.program_id(1)

    @pl.when(c == 0)
    def _():
        acc_scr[...] = jnp.zeros_like(acc_scr)

    ht = jnp.dot(u_ref[...], xnt_ref[...], preferred_element_type=F32)
    for e in range(E1_PER_CHUNK):
        e1 = c * E1_PER_CHUNK + e
        wgt = None
        for h in range(PEER_HEADS):
            c1row = c1_ref[h, pl.ds(e1, 1), :]
            c2row = c2_ref[h, pl.ds(e1, 1), :]
            s2 = s2_ref[h]
            term = jnp.where(s2 >= c1row, jnp.exp(s2 - c2row), 0.0)
            wgt = term if wgt is None else wgt + term
        rows = slice(e * N_KEYS, (e + 1) * N_KEYS)
        a_scr[rows, :] = (_gelu(ht[rows, :]) * wgt).astype(BF16)
    acc_scr[...] += jnp.dot(vt_ref[...], a_scr[...], preferred_element_type=F32)

    @pl.when(c == pl.num_programs(1) - 1)
    def _():
        out_o[...] = acc_scr[...].T


def _peer(xnt, u_bf16, vt_bf16, c1, s2, c2, tile):
    rows = xnt.shape[1]
    body = functools.partial(_peer_body, tile=tile)
    sco = pl.BlockSpec((PEER_HEADS, N_KEYS, tile), lambda i, c: (0, 0, i))
    return pl.pallas_call(
        body,
        grid=(rows // tile, N_EXPERTS // CHUNK),
        in_specs=[
            pl.BlockSpec((D_MODEL, tile), lambda i, c: (0, i)),
            pl.BlockSpec((CHUNK, D_MODEL), lambda i, c: (c, 0)),
            pl.BlockSpec((D_MODEL, CHUNK), lambda i, c: (0, c)),
            sco, sco, sco,
        ],
        out_specs=pl.BlockSpec((tile, D_MODEL), lambda i, c: (i, 0)),
        out_shape=jax.ShapeDtypeStruct((rows, D_MODEL), F32),
        scratch_shapes=[pltpu.VMEM((D_MODEL, tile), F32), pltpu.VMEM((CHUNK, tile), BF16)],
        compiler_params=_cparams("parallel", "arbitrary"),
        name="peer",
    )(xnt, u_bf16, vt_bf16, c1, s2, c2)


def _final_body(h_ref, peer_ref, p_ref, npg_ref, gatew_ref, plew_ref, fg_ref, y_o):
    h = h_ref[...] + peer_ref[...]
    gate = _sigmoid(_bdot(_rms(h, npg_ref[...]), gatew_ref[...]))
    h = h + _bdot(p_ref[...], plew_ref[...]) * gate
    y_o[...] = _rms(h, fg_ref[...])


def _final(h, peer, p2d, lw, gate_w_bf16, ple_w_bf16, final_g, tile):
    rows = h.shape[0]
    full = lambda shape: pl.BlockSpec(shape, lambda i: (0,) * len(shape))
    wide = pl.BlockSpec((tile, D_MODEL), lambda i: (i, 0))
    return pl.pallas_call(
        _final_body,
        grid=(rows // tile,),
        in_specs=[wide, wide, pl.BlockSpec((tile, PLE_DIM), lambda i: (i, 0)), full((1, D_MODEL)),
                  full((D_MODEL, D_MODEL)), full((PLE_DIM, D_MODEL)), full((1, D_MODEL))],
        out_specs=wide,
        out_shape=jax.ShapeDtypeStruct((rows, D_MODEL), F32),
        compiler_params=_cparams("parallel"),
        name="final",
    )(h, peer, p2d, lw["norm_ple_g"], gate_w_bf16, ple_w_bf16, final_g)


def _block_diag_ones():
    seg = jnp.arange(D_RWKV) // RWKV_HEAD
    return (seg[:, None] == seg[None, :]).astype(BF16)


def _tail(x2d, p2d, o, bonus, g, yp, lw, wts, bd, final_g, tile):
    h, xnt = _postmix(x2d, o, bonus, g, yp, lw, wts["w_out"], bd, tile)
    c1, s2, c2 = _route(xnt, wts["wqt"], wts["keys"], tile)
    peer = _peer(xnt, wts["u"], wts["vt"], c1, s2, c2, tile)
    return _final(h, peer, p2d, lw, wts["gate_w"], wts["ple_w"], final_g, tile)


def kernel(x_prompt, x_sample, state_shift, state_wkv, state_pool, p_prompt, p_sample, norm_mix_g, w_in, shift_mu, decay_w0, decay_b, a_0, a_b, g_b, k_k, k_a, r_k, lnx_g, lnx_b, pool_w, pool_scale, w_out, norm_ffn_g, peer_wq, peer_keys, peer_u, peer_v, norm_ple_g, ple_w, ple_gate_w, final_norm_g):
    assert norm_mix_g.shape[0] == 1, "single trunk layer"
    bsz, seq, _ = x_prompt.shape
    dbsz, dseq, _ = x_sample.shape
    row = lambda a: a.reshape(1, -1)
    lw = dict(
        shift_mu=row(shift_mu[0]), decay_w0=row(decay_w0[0]), decay_b=decay_b[0], a_0=row(a_0[0]), a_b=a_b[0],
        g_b=g_b[0], k_k=row(k_k[0]), k_a=row(k_a[0]), r_k=row(r_k[0]), lnx_g=row(lnx_g[0]), lnx_b=row(lnx_b[0]),
        pool_w=pool_w[0], pool_scale=row(pool_scale[0]), norm_ffn_g=row(norm_ffn_g[0]),
        norm_ple_g=row(norm_ple_g[0]))
    wts = dict(
        w_out=w_out[0].astype(BF16), wqt=peer_wq[0].T.astype(BF16), keys=peer_keys[0].astype(BF16),
        u=peer_u[0].astype(BF16), vt=peer_v[0].T.astype(BF16), gate_w=ple_gate_w[0].astype(BF16),
        ple_w=ple_w[0].astype(BF16))
    w_in_bf16 = w_in[0].astype(BF16)
    mix_g = row(norm_mix_g[0])
    final_g = row(final_norm_g)
    bd = _block_diag_ones()
    tile = 512

    xp = x_prompt.reshape(bsz * seq, D_MODEL)
    zp = _proj(xp, mix_g, w_in_bf16, tile)
    pm_tile = 256
    r, w, k, v, kk, b, g, bonus, yp = _premix(zp, lw, bd, pm_tile, seq // pm_tile)
    seq3 = lambda a: a.reshape(D_RWKV, bsz, seq)
    s0 = jnp.zeros((bsz, N_RWKV_HEADS, RWKV_HEAD, RWKV_HEAD), F32)
    o, s_fin = _scan(seq3(r), seq3(w), seq3(k), seq3(v), seq3(kk), seq3(b), s0, 64)
    y_prompt = _tail(xp, p_prompt[0].reshape(bsz * seq, PLE_DIM), o, bonus, g, yp,
                     lw, wts, bd, final_g, tile).reshape(bsz, seq, D_MODEL)
    zp3 = zp.reshape(bsz, seq, D_IN)
    shift_prompt = zp3[:, -1, :D_SHIFT][None]
    wkv_prompt = s_fin[None]
    pool_prompt = zp3[:, seq - POOL_BUF:, D_SHIFT:][None]

    xs = x_sample.reshape(dbsz * dseq, D_MODEL)
    zs = _proj(xs, mix_g, w_in_bf16, dbsz * dseq).reshape(dbsz, dseq, D_IN)
    hist = jnp.zeros((dbsz, HALO, D_IN), F32)
    hist = hist.at[:, HALO - POOL_BUF:, D_SHIFT:].set(state_pool[0])
    hist = hist.at[:, HALO - 1, :D_SHIFT].set(state_shift[0])
    ext_len = HALO + dseq
    z_ext = jnp.concatenate([hist, zs], axis=1).reshape(dbsz * ext_len, D_IN)
    outs = _premix(z_ext, lw, bd, 512, 0)
    keep = lambda a: a.reshape(dbsz, ext_len, D_RWKV)[:, HALO:, :]
    keep_t = lambda a: a.reshape(D_RWKV, dbsz, ext_len)[:, :, HALO:]
    r, w, k, v, kk, b = [keep_t(a) for a in outs[:6]]
    g, bonus, yp = [keep(a) for a in outs[6:]]
    o, s_fin = _scan(r, w, k, v, kk, b, state_wkv[0], dseq)
    flat = lambda a: a.reshape(dbsz * dseq, D_RWKV)
    y_sample = _tail(xs, p_sample[0].reshape(dbsz * dseq, PLE_DIM), o, flat(bonus), flat(g), flat(yp),
                     lw, wts, bd, final_g, tile).reshape(dbsz, dseq, D_MODEL)
    shift_sample = zs[:, -1, :D_SHIFT][None]
    wkv_sample = s_fin[None]
    pool_sample = jnp.concatenate([state_pool[0], zs[:, :, D_SHIFT:]], axis=1)[:, -POOL_BUF:][None]

    return (y_prompt, y_sample, shift_prompt, wkv_prompt, pool_prompt, shift_sample, wkv_sample, pool_sample)
```

```python
import functools
import math

import jax
import jax.numpy as jnp
from jax import lax
from jax.experimental import pallas as pl
from jax.experimental.pallas import tpu as pltpu

F32 = jnp.float32
BF16 = jnp.bfloat16

D_MODEL = 1024
RWKV_HEAD = 64
D_RWKV = 512
N_RWKV_HEADS = 8
DECAY_LORA = 64
AAA_LORA = 64
GATE_LORA = 128
D_SHIFT = 3 * D_RWKV + DECAY_LORA + AAA_LORA + GATE_LORA
D_POOL = 512
POOL_WINDOWS = (2, 4, 8, 16)
POOL_GROUP = 128
POOL_BUF = 15
D_IN = D_SHIFT + D_POOL
N_KEYS = 128
N_EXPERTS = N_KEYS * N_KEYS
PEER_HEADS = 8
PEER_TOPK = 16
PEER_DK = 256
PEER_DK_HALF = 128
PLE_DIM = 256
NORM_EPS = 1e-6
LNX_EPS = 64e-5
PAST_LEN = 16384

HALO = 16
LANES = 128
SUBLANES = 8
VMEM_LIMIT = 48 * 1024 * 1024
TOKEN_TILE = 512
PREMIX_TILE = 256
SCAN_STEPS = 64
assert PAST_LEN >= POOL_BUF

NEG_INF = float("-inf")


def _cparams(*sem):
    return pltpu.CompilerParams(dimension_semantics=sem, vmem_limit_bytes=VMEM_LIMIT)


def _rms(x, g):
    return x * lax.rsqrt(jnp.mean(x * x, axis=-1, keepdims=True) + NORM_EPS) * g


def _sigmoid(x):
    return 1.0 / (1.0 + jnp.exp(-x))


def _bdot(a, b):
    return jnp.dot(a.astype(BF16), b.astype(BF16), preferred_element_type=F32)


def _seg_sum(x, bd):
    x1 = x.astype(BF16)
    r1 = x - x1.astype(F32)
    x2 = r1.astype(BF16)
    x3 = (r1 - x2.astype(F32)).astype(BF16)
    dot = functools.partial(jnp.dot, preferred_element_type=F32)
    return dot(x1, bd) + dot(x2, bd) + dot(x3, bd)


def _proj_body(x_ref, g_ref, w_ref, z_ref):
    z_ref[...] = _bdot(_rms(x_ref[...], g_ref[...]), w_ref[...])


def _proj(x2d, g, w_bf16, tile):
    rows = x2d.shape[0]
    return pl.pallas_call(
        _proj_body,
        grid=(rows // tile,),
        in_specs=[
            pl.BlockSpec((tile, D_MODEL), lambda i: (i, 0)),
            pl.BlockSpec((1, D_MODEL), lambda i: (0, 0)),
            pl.BlockSpec((D_MODEL, D_IN), lambda i: (0, 0)),
        ],
        out_specs=pl.BlockSpec((tile, D_IN), lambda i: (i, 0)),
        out_shape=jax.ShapeDtypeStruct((rows, D_IN), F32),
        compiler_params=_cparams("parallel"),
        name="proj",
    )(x2d, g, w_bf16)


def _premix_body(z_ref, halo_ref, mu_ref, w0_ref, decb_ref, a0_ref, ab_ref, gb_ref, kk_ref, ka_ref,
                 rk_ref, poolw_ref, pools_ref, bd_ref,
                 r_o, w_o, k_o, v_o, kkn_o, b_o, g_o, bonus_o, yp_o, ext_scr, *, tile, seq_tiles):
    halo = halo_ref[...]
    if seq_tiles:
        ti = pl.program_id(0) % seq_tiles
        halo = jnp.where(ti == 0, 0.0, halo)
    ext_scr[0:HALO, :] = halo
    ext_scr[HALO:HALO + tile, :] = z_ref[...]

    zr = z_ref[:, :D_SHIFT]
    zp = ext_scr[HALO - 1:HALO - 1 + tile, :D_SHIFT]
    zs = zr + (zp - zr) * mu_ref[...]
    r = zs[:, :D_RWKV]
    k = zs[:, D_RWKV:2 * D_RWKV]
    v = zs[:, 2 * D_RWKV:3 * D_RWKV]
    o = 3 * D_RWKV
    zw = zs[:, o:o + DECAY_LORA]
    za = zs[:, o + DECAY_LORA:o + DECAY_LORA + AAA_LORA]
    zg = zs[:, o + DECAY_LORA + AAA_LORA:]
    dec_in = w0_ref[...] + _bdot(jnp.tanh(zw), decb_ref[...])
    nx = -dec_in
    softplus = jnp.maximum(nx, 0.0) + jnp.log1p(jnp.exp(-jnp.abs(nx)))
    decay = jnp.exp(-jnp.exp(-softplus - 0.5))
    a = _sigmoid(a0_ref[...] + _bdot(za, ab_ref[...]))
    g = _bdot(_sigmoid(zg), gb_ref[...])
    bd = bd_ref[...]
    kkf = k * kk_ref[...]
    kkn = kkf / jnp.maximum(jnp.sqrt(_seg_sum(kkf * kkf, bd)), 1e-12)
    k2 = k * (1.0 + (a - 1.0) * ka_ref[...])
    bonus = _seg_sum(r * k2 * rk_ref[...], bd) * v
    r_o[...] = r.T
    w_o[...] = decay.T
    k_o[...] = k2.T
    v_o[...] = v.T
    kkn_o[...] = kkn.T
    b_o[...] = (kkn * a).T
    g_o[...] = g
    bonus_o[...] = bonus

    if seq_tiles:
        pos = ti * tile + lax.broadcasted_iota(jnp.int32, (tile, 1), 0)
    for gi, wdw in enumerate(POOL_WINDOWS):
        cols = slice(D_SHIFT + gi * POOL_GROUP, D_SHIFT + (gi + 1) * POOL_GROUP)
        u_ext = ext_scr[:, cols]
        s = u_ext
        shift = 1
        while shift < wdw:
            s = s + pltpu.roll(s, shift, 0)
            shift *= 2
        u = u_ext[HALO:, :]
        if seq_tiles:
            cnt = jnp.minimum(pos + 1, wdw).astype(F32)
        else:
            cnt = float(wdw)
        pooled = s[HALO:, :] / cnt - u
        y = _bdot(pooled, poolw_ref[gi])
        yp_o[:, gi * POOL_GROUP:(gi + 1) * POOL_GROUP] = y * pools_ref[:, gi * POOL_GROUP:(gi + 1) * POOL_GROUP]


def _premix(z, lw, bd, tile, seq_tiles):
    rows = z.shape[0]
    body = functools.partial(_premix_body, tile=tile, seq_tiles=seq_tiles)
    full = lambda shape: pl.BlockSpec(shape, lambda i: (0,) * len(shape))
    out_block = pl.BlockSpec((tile, D_RWKV), lambda i: (i, 0))
    out = jax.ShapeDtypeStruct((rows, D_RWKV), F32)
    out_block_t = pl.BlockSpec((D_RWKV, tile), lambda i: (0, i))
    out_t = jax.ShapeDtypeStruct((D_RWKV, rows), F32)
    transposed = (True, True, True, True, True, True, False, False, False)
    halo_blocks = tile // HALO
    return pl.pallas_call(
        body,
        grid=(rows // tile,),
        in_specs=[
            pl.BlockSpec((tile, D_IN), lambda i: (i, 0)),
            pl.BlockSpec((HALO, D_IN), lambda i: (jnp.maximum(i * halo_blocks - 1, 0), 0)),
            full((1, D_SHIFT)), full((1, D_RWKV)), full((DECAY_LORA, D_RWKV)), full((1, D_RWKV)),
            full((AAA_LORA, D_RWKV)), full((GATE_LORA, D_RWKV)), full((1, D_RWKV)), full((1, D_RWKV)),
            full((1, D_RWKV)), full((4, POOL_GROUP, POOL_GROUP)), full((1, D_POOL)), full((D_RWKV, D_RWKV)),
        ],
        out_specs=[out_block_t if t else out_block for t in transposed],
        out_shape=[out_t if t else out for t in transposed],
        scratch_shapes=[pltpu.VMEM((tile + HALO, D_IN), F32)],
        compiler_params=_cparams("parallel"),
        name="premix",
    )(z, z, lw["shift_mu"], lw["decay_w0"], lw["decay_b"], lw["a_0"], lw["a_b"], lw["g_b"], lw["k_k"],
      lw["k_a"], lw["r_k"], lw["pool_w"], lw["pool_scale"], bd)


GROUP_SEQS = LANES // (2 * N_RWKV_HEADS)
HALF_ROWS = RWKV_HEAD // 2
N_ACC = 4


def _tree_sum(parts):
    while len(parts) > 1:
        parts = [a + b for a, b in zip(parts[::2], parts[1::2])]
    return parts[0]


def _scan_body(w_ref, kk_ref, b_ref, k_ref, r_ref, v_ref, s0_ref, o_ref, sout_ref, st_scr, *, tc):
    c = pl.program_id(1)

    @pl.when(c == 0)
    def _():
        st_scr[...] = s0_ref[0]

    def step(t, carry):
        acc = [None] * N_ACC
        for j in range(RWKV_HEAD):
            term = st_scr[j] * kk_ref[0, t, pl.ds(j, 1), :]
            acc[j % N_ACC] = term if acc[j % N_ACC] is None else acc[j % N_ACC] + term
        sk = _tree_sum(acc)
        v = v_ref[0, t]
        acc = [None] * N_ACC
        for j in range(RWKV_HEAD):
            row_j = pl.ds(j, 1)
            s = (st_scr[j] * w_ref[0, t, row_j, :] - sk * b_ref[0, t, row_j, :]) + v * k_ref[0, t, row_j, :]
            st_scr[j] = s
            term = s * r_ref[0, t, row_j, :]
            acc[j % N_ACC] = term if acc[j % N_ACC] is None else acc[j % N_ACC] + term
        o_ref[0, t] = _tree_sum(acc)
        return carry

    if tc < SUBLANES:
        for t in range(tc):
            step(t, 0)
    else:
        lax.fori_loop(0, tc, step, 0)

    @pl.when(c == pl.num_programs(1) - 1)
    def _():
        sout_ref[0] = st_scr[...]


def _keys_to_lanes(xt):
    _, bsz, t_len = xt.shape
    y = xt.reshape(1, N_RWKV_HEADS, RWKV_HEAD, bsz // GROUP_SEQS, GROUP_SEQS, t_len)
    y = jnp.broadcast_to(y, (2, N_RWKV_HEADS, RWKV_HEAD, bsz // GROUP_SEQS, GROUP_SEQS, t_len))
    return y.transpose(3, 5, 2, 0, 4, 1).reshape(bsz // GROUP_SEQS, t_len, RWKV_HEAD, LANES)


def _values_to_lanes(xt):
    _, bsz, t_len = xt.shape
    y = xt.reshape(N_RWKV_HEADS, 2, HALF_ROWS, bsz // GROUP_SEQS, GROUP_SEQS, t_len).transpose(3, 5, 2, 1, 4, 0)
    return y.reshape(bsz // GROUP_SEQS, t_len, HALF_ROWS, LANES)


def _values_from_lanes(y):
    groups, t_len = y.shape[:2]
    y = y.reshape(groups, t_len, HALF_ROWS, 2, GROUP_SEQS, N_RWKV_HEADS).transpose(5, 3, 2, 0, 4, 1)
    return y.reshape(D_RWKV, groups * GROUP_SEQS * t_len)


def _state_to_lanes(s):
    bsz = s.shape[0]
    y = s.reshape(bsz // GROUP_SEQS, GROUP_SEQS, N_RWKV_HEADS, 2, HALF_ROWS, RWKV_HEAD).transpose(0, 5, 4, 3, 1, 2)
    return y.reshape(bsz // GROUP_SEQS, RWKV_HEAD, HALF_ROWS, LANES)


def _state_from_lanes(y):
    groups = y.shape[0]
    y = y.reshape(groups, RWKV_HEAD, HALF_ROWS, 2, GROUP_SEQS, N_RWKV_HEADS).transpose(0, 4, 5, 3, 2, 1)
    return y.reshape(groups * GROUP_SEQS, N_RWKV_HEADS, RWKV_HEAD, RWKV_HEAD)


def _scan(r, w, k, v, kk, b, s0, tc):
    _, bsz, t_len = v.shape
    groups = bsz // GROUP_SEQS
    body = functools.partial(_scan_body, tc=tc)
    keys = pl.BlockSpec((1, tc, RWKV_HEAD, LANES), lambda i, c: (i, c, 0, 0))
    vals = pl.BlockSpec((1, tc, HALF_ROWS, LANES), lambda i, c: (i, c, 0, 0))
    st = pl.BlockSpec((1, RWKV_HEAD, HALF_ROWS, LANES), lambda i, c: (i, 0, 0, 0))
    o, s_fin = pl.pallas_call(
        body,
        grid=(groups, t_len // tc),
        in_specs=[keys, keys, keys, keys, keys, vals, st],
        out_specs=[vals, st],
        out_shape=[jax.ShapeDtypeStruct((groups, t_len, HALF_ROWS, LANES), F32),
                   jax.ShapeDtypeStruct((groups, RWKV_HEAD, HALF_ROWS, LANES), F32)],
        scratch_shapes=[pltpu.VMEM((RWKV_HEAD, HALF_ROWS, LANES), F32)],
        compiler_params=_cparams("parallel", "arbitrary"),
        name="scan",
    )(_keys_to_lanes(w), _keys_to_lanes(kk), _keys_to_lanes(b), _keys_to_lanes(k), _keys_to_lanes(r),
      _values_to_lanes(v), _state_to_lanes(s0))
    return _values_from_lanes(o), _state_from_lanes(s_fin)


def _postmix_body(x_ref, o_ref, bonus_ref, g_ref, yp_ref, lng_ref, lnb_ref, wout_ref, nfg_ref, bd_ref,
                  h_o, xnt_o):
    bd = bd_ref[...]
    o = o_ref[...].T
    inv_n = 1.0 / RWKV_HEAD
    d = o - _seg_sum(o, bd) * inv_n
    var = _seg_sum(d * d, bd) * inv_n
    y_r = (d * lax.rsqrt(var + LNX_EPS) * lng_ref[...] + lnb_ref[...] + bonus_ref[...]) * g_ref[...]
    h = x_ref[...] + _bdot(y_r, wout_ref[:D_RWKV, :]) + _bdot(yp_ref[...], wout_ref[D_RWKV:, :])
    h_o[...] = h
    xnt_o[...] = _rms(h, nfg_ref[...]).T.astype(BF16)


def _postmix(x2d, o, bonus, g, yp, lw, w_out_bf16, bd, tile):
    rows = x2d.shape[0]
    full = lambda shape: pl.BlockSpec(shape, lambda i: (0,) * len(shape))
    half = pl.BlockSpec((tile, D_RWKV), lambda i: (i, 0))
    wide = pl.BlockSpec((tile, D_MODEL), lambda i: (i, 0))
    return pl.pallas_call(
        _postmix_body,
        grid=(rows // tile,),
        in_specs=[wide, pl.BlockSpec((D_RWKV, tile), lambda i: (0, i)), half, half, half,
                  full((1, D_RWKV)), full((1, D_RWKV)),
                  full((D_MODEL, D_MODEL)), full((1, D_MODEL)), full((D_RWKV, D_RWKV))],
        out_specs=[wide, pl.BlockSpec((D_MODEL, tile), lambda i: (0, i))],
        out_shape=[jax.ShapeDtypeStruct((rows, D_MODEL), F32), jax.ShapeDtypeStruct((D_MODEL, rows), BF16)],
        compiler_params=_cparams("parallel"),
        name="postmix",
    )(x2d, o, bonus, g, yp, lw["lnx_g"], lw["lnx_b"], w_out_bf16, lw["norm_ffn_g"], bd)


def _oddeven_mergesort_pairs(n):
    pairs = []
    p = 1
    while p < n:
        k = p
        while k >= 1:
            for j in range(k % p, n - k, 2 * k):
                for i in range(min(k, n - j - k)):
                    if (i + j) // (2 * p) == (i + j + k) // (2 * p):
                        pairs.append((i + j, i + j + k))
            k //= 2
        p *= 2
    return pairs


_SORT16 = _oddeven_mergesort_pairs(PEER_TOPK)


def _exchange(lst, i, j):
    lst[i], lst[j] = jnp.maximum(lst[i], lst[j]), jnp.minimum(lst[i], lst[j])


def _merge_top16(a, b):
    c = [jnp.maximum(a[i], b[PEER_TOPK - 1 - i]) for i in range(PEER_TOPK)]
    stride = PEER_TOPK // 2
    while stride:
        for i in range(PEER_TOPK):
            if not i & stride:
                _exchange(c, i, i + stride)
        stride //= 2
    return c


def _top16_of_keys(s):
    lst = [s[SUBLANES * g:SUBLANES * (g + 1), :] for g in range(N_KEYS // SUBLANES)]
    for i, j in _SORT16:
        _exchange(lst, i, j)
    shift = SUBLANES // 2
    while shift:
        lst = _merge_top16(lst, [pltpu.roll(x, shift, 0) for x in lst])
        shift //= 2
    return lst


def _route_body(xnt_ref, wq_ref, keys_ref, s1_o, s2_o, c2_o, *, tile):
    qt = lax.dot_general(wq_ref[...], xnt_ref[...], (((0,), (0,)), ((), ())), preferred_element_type=F32).astype(BF16)
    head = lax.broadcasted_iota(jnp.int32, (PEER_HEADS, tile), 0)
    neg = jnp.full((PEER_HEADS, tile), NEG_INF, F32)
    top = [[neg] * PEER_TOPK, [neg] * PEER_TOPK]
    for h in range(PEER_HEADS):
        base = h * PEER_DK
        for half, s_o in enumerate((s1_o, s2_o)):
            q_half = qt[base + half * PEER_DK_HALF:base + (half + 1) * PEER_DK_HALF]
            s = jnp.dot(keys_ref[half], q_half, preferred_element_type=F32)
            s_o[h] = s
            best = _top16_of_keys(s)
            top[half] = [jnp.where(head == h, best[a], top[half][a]) for a in range(PEER_TOPK)]
    v1, v2 = top
    pad = lambda lst: lst + [neg] * (PEER_TOPK - len(lst))
    sc = [v1[0] + v2[b] for b in range(PEER_TOPK)]
    for a in range(1, PEER_TOPK // 2):
        sc = _merge_top16(sc, pad([v1[a] + v2[b] for b in range(PEER_TOPK // (a + 1))]))
    sc = _merge_top16(sc, pad([v1[a] + v2[0] for a in range(PEER_TOPK // 2, PEER_TOPK)]))
    z = jnp.ones_like(sc[0])
    for kth in sc[1:]:
        z = z + jnp.exp(kth - sc[0])
    tau = sc[PEER_TOPK - 1]
    lse = sc[0] + jnp.log(z)
    for h in range(PEER_HEADS):
        row = slice(h, h + 1)
        s1 = s1_o[h]
        c2_o[h] = lse[row] - s1
        s1_o[h] = tau[row] - s1


def _route(xnt, wq_bf16, keys_bf16, tile):
    rows = xnt.shape[1]
    body = functools.partial(_route_body, tile=tile)
    sco = pl.BlockSpec((PEER_HEADS, N_KEYS, tile), lambda i: (0, 0, i))
    sc_shape = jax.ShapeDtypeStruct((PEER_HEADS, N_KEYS, rows), F32)
    return pl.pallas_call(
        body,
        grid=(rows // tile,),
        in_specs=[
            pl.BlockSpec((D_MODEL, tile), lambda i: (0, i)),
            pl.BlockSpec((D_MODEL, PEER_HEADS * PEER_DK), lambda i: (0, 0)),
            pl.BlockSpec((2, N_KEYS, PEER_DK_HALF), lambda i: (0, 0, 0)),
        ],
        out_specs=[sco, sco, sco],
        out_shape=[sc_shape, sc_shape, sc_shape],
        compiler_params=_cparams("parallel"),
        name="route",
    )(xnt, wq_bf16, keys_bf16)


E1_PER_CHUNK = 8
CHUNK = E1_PER_CHUNK * N_KEYS


def _gelu(x):
    return 0.5 * x * (1.0 + lax.erf(x * (1.0 / math.sqrt(2.0))))


def _peer_body(xnt_ref, u_ref, v_ref, c1_ref, s2_ref, c2_ref, out_o, acc_scr, a_scr, *, tile):
    c = pl.program_id(1)

    @pl.when(c == 0)
    def _():
        acc_scr[...] = jnp.zeros_like(acc_scr)

    ht = jnp.dot(u_ref[...], xnt_ref[...], preferred_element_type=F32)
    for e in range(E1_PER_CHUNK):
        e1 = c * E1_PER_CHUNK + e
        wgt = None
        for h in range(PEER_HEADS):
            c1row = c1_ref[h, pl.ds(e1, 1), :]
            c2row = c2_ref[h, pl.ds(e1, 1), :]
            s2 = s2_ref[h]
            term = jnp.where(s2 >= c1row, jnp.exp(s2 - c2row), 0.0)
            wgt = term if wgt is None else wgt + term
        rows = slice(e * N_KEYS, (e + 1) * N_KEYS)
        a_scr[rows, :] = (_gelu(ht[rows, :]) * wgt).astype(BF16)
    acc_scr[...] += lax.dot_general(v_ref[...], a_scr[...], (((0,), (0,)), ((), ())), preferred_element_type=F32)

    @pl.when(c == pl.num_programs(1) - 1)
    def _():
        out_o[...] = acc_scr[...].T


def _peer(xnt, u_bf16, v_bf16, c1, s2, c2, tile):
    rows = xnt.shape[1]
    body = functools.partial(_peer_body, tile=tile)
    sco = pl.BlockSpec((PEER_HEADS, N_KEYS, tile), lambda i, c: (0, 0, i))
    return pl.pallas_call(
        body,
        grid=(rows // tile, N_EXPERTS // CHUNK),
        in_specs=[
            pl.BlockSpec((D_MODEL, tile), lambda i, c: (0, i)),
            pl.BlockSpec((CHUNK, D_MODEL), lambda i, c: (c, 0)),
            pl.BlockSpec((CHUNK, D_MODEL), lambda i, c: (c, 0)),
            sco, sco, sco,
        ],
        out_specs=pl.BlockSpec((tile, D_MODEL), lambda i, c: (i, 0)),
        out_shape=jax.ShapeDtypeStruct((rows, D_MODEL), F32),
        scratch_shapes=[pltpu.VMEM((D_MODEL, tile), F32), pltpu.VMEM((CHUNK, tile), BF16)],
        compiler_params=_cparams("parallel", "arbitrary"),
        name="peer",
    )(xnt, u_bf16, v_bf16, c1, s2, c2)


def _final_body(h_ref, peer_ref, p_ref, npg_ref, gatew_ref, plew_ref, fg_ref, y_o):
    h = h_ref[...] + peer_ref[...]
    gate = _sigmoid(_bdot(_rms(h, npg_ref[...]), gatew_ref[...]))
    h = h + _bdot(p_ref[...], plew_ref[...]) * gate
    y_o[...] = _rms(h, fg_ref[...])


def _final(h, peer, p2d, lw, gate_w_bf16, ple_w_bf16, final_g, tile):
    rows = h.shape[0]
    full = lambda shape: pl.BlockSpec(shape, lambda i: (0,) * len(shape))
    wide = pl.BlockSpec((tile, D_MODEL), lambda i: (i, 0))
    return pl.pallas_call(
        _final_body,
        grid=(rows // tile,),
        in_specs=[wide, wide, pl.BlockSpec((tile, PLE_DIM), lambda i: (i, 0)), full((1, D_MODEL)),
                  full((D_MODEL, D_MODEL)), full((PLE_DIM, D_MODEL)), full((1, D_MODEL))],
        out_specs=wide,
        out_shape=jax.ShapeDtypeStruct((rows, D_MODEL), F32),
        compiler_params=_cparams("parallel"),
        name="final",
    )(h, peer, p2d, lw["norm_ple_g"], gate_w_bf16, ple_w_bf16, final_g)


def _block_diag_ones():
    seg = jnp.arange(D_RWKV) // RWKV_HEAD
    return (seg[:, None] == seg[None, :]).astype(BF16)


def _tail(x2d, p2d, o, bonus, g, yp, lw, wts, bd, final_g, tile):
    h, xnt = _postmix(x2d, o, bonus, g, yp, lw, wts["w_out"], bd, tile)
    c1, s2, c2 = _route(xnt, wts["wq"], wts["keys"], tile)
    peer = _peer(xnt, wts["u"], wts["v"], c1, s2, c2, tile)
    return _final(h, peer, p2d, lw, wts["gate_w"], wts["ple_w"], final_g, tile)


def kernel(x_prompt, x_sample, state_shift, state_wkv, state_pool, p_prompt, p_sample, norm_mix_g, w_in, shift_mu, decay_w0, decay_b, a_0, a_b, g_b, k_k, k_a, r_k, lnx_g, lnx_b, pool_w, pool_scale, w_out, norm_ffn_g, peer_wq, peer_keys, peer_u, peer_v, norm_ple_g, ple_w, ple_gate_w, final_norm_g):
    assert norm_mix_g.shape[0] == 1, "single trunk layer"
    bsz, seq, _ = x_prompt.shape
    dbsz, dseq, _ = x_sample.shape
    row = lambda a: a.reshape(1, -1)
    lw = dict(
        shift_mu=row(shift_mu[0]), decay_w0=row(decay_w0[0]), decay_b=decay_b[0], a_0=row(a_0[0]), a_b=a_b[0],
        g_b=g_b[0], k_k=row(k_k[0]), k_a=row(k_a[0]), r_k=row(r_k[0]), lnx_g=row(lnx_g[0]), lnx_b=row(lnx_b[0]),
        pool_w=pool_w[0], pool_scale=row(pool_scale[0]), norm_ffn_g=row(norm_ffn_g[0]),
        norm_ple_g=row(norm_ple_g[0]))
    wts = dict(
        w_out=w_out[0].astype(BF16), wq=peer_wq[0].astype(BF16), keys=peer_keys[0].astype(BF16),
        u=peer_u[0].astype(BF16), v=peer_v[0].astype(BF16), gate_w=ple_gate_w[0].astype(BF16),
        ple_w=ple_w[0].astype(BF16))
    w_in_bf16 = w_in[0].astype(BF16)
    mix_g = row(norm_mix_g[0])
    final_g = row(final_norm_g)
    bd = _block_diag_ones()
    tile = TOKEN_TILE

    xp = x_prompt.reshape(bsz * seq, D_MODEL)
    zp = _proj(xp, mix_g, w_in_bf16, tile)
    r, w, k, v, kk, b, g, bonus, yp = _premix(zp, lw, bd, PREMIX_TILE, seq // PREMIX_TILE)
    seq3 = lambda a: a.reshape(D_RWKV, bsz, seq)
    s0 = jnp.zeros((bsz, N_RWKV_HEADS, RWKV_HEAD, RWKV_HEAD), F32)
    o, s_fin = _scan(seq3(r), seq3(w), seq3(k), seq3(v), seq3(kk), seq3(b), s0, SCAN_STEPS)
    y_prompt = _tail(xp, p_prompt[0].reshape(bsz * seq, PLE_DIM), o, bonus, g, yp,
                     lw, wts, bd, final_g, tile).reshape(bsz, seq, D_MODEL)
    zp3 = zp.reshape(bsz, seq, D_IN)
    shift_prompt = zp3[:, -1, :D_SHIFT][None]
    wkv_prompt = s_fin[None]
    pool_prompt = zp3[:, seq - POOL_BUF:, D_SHIFT:][None]

    xs = x_sample.reshape(dbsz * dseq, D_MODEL)
    zs = _proj(xs, mix_g, w_in_bf16, dbsz * dseq).reshape(dbsz, dseq, D_IN)
    hist = jnp.zeros((dbsz, HALO, D_IN), F32)
    hist = hist.at[:, HALO - POOL_BUF:, D_SHIFT:].set(state_pool[0])
    hist = hist.at[:, HALO - 1, :D_SHIFT].set(state_shift[0])
    ext_len = HALO + dseq
    z_ext = jnp.concatenate([hist, zs], axis=1).reshape(dbsz * ext_len, D_IN)
    outs = _premix(z_ext, lw, bd, tile, 0)
    keep = lambda a: a.reshape(dbsz, ext_len, D_RWKV)[:, HALO:, :]
    keep_t = lambda a: a.reshape(D_RWKV, dbsz, ext_len)[:, :, HALO:]
    r, w, k, v, kk, b = [keep_t(a) for a in outs[:6]]
    g, bonus, yp = [keep(a) for a in outs[6:]]
    o, s_fin = _scan(r, w, k, v, kk, b, state_wkv[0], dseq)
    flat = lambda a: a.reshape(dbsz * dseq, D_RWKV)
    y_sample = _tail(xs, p_sample[0].reshape(dbsz * dseq, PLE_DIM), o, flat(bonus), flat(g), flat(yp),
                     lw, wts, bd, final_g, tile).reshape(dbsz, dseq, D_MODEL)
    shift_sample = zs[:, -1, :D_SHIFT][None]
    wkv_sample = s_fin[None]
    pool_sample = jnp.concatenate([state_pool[0], zs[:, :, D_SHIFT:]], axis=1)[:, -POOL_BUF:][None]

    return (y_prompt, y_sample, shift_prompt, wkv_prompt, pool_prompt, shift_sample, wkv_sample, pool_sample)
```
